```python
import jax, jax.numpy as jnp
from jax import lax
import numpy as np

D_MODEL = 2048
BATCH = 4
SEQ = 4096
DEPTH = 2

D_MIX = D_MODEL
HEAD_DIM = 64
D_ATTN = D_MIX // 2
N_HEADS = D_ATTN // HEAD_DIM
N_KV_HEADS = N_HEADS // 8
WINDOW = 128
BLOCK = WINDOW
D_CONV = D_MIX // 4
CONV_WIDTH = 3
D_LRU = D_MIX // 4
LRU_HEAD_DIM = 64
LRU_HEADS = D_LRU // LRU_HEAD_DIM
LRU_CONV_WIDTH = 4
LRU_C = 8.0
D_KV = N_KV_HEADS * HEAD_DIM
D_IN = D_ATTN + 2 * D_KV + 3 * D_CONV + 2 * D_LRU
D_FF = 5632
N_EXPERTS = 8
TOP_K = 2
D_EXPERT = D_FF // 2
N_DENSE = (DEPTH + 1) // 2
N_MOE = DEPTH // 2
EPS = 1e-6

kernel_name = "hymba_style_hybrid_swa_conv_rglru_moe"


def rmsnorm(x, g):
    xf = x.astype(jnp.float32)
    y = xf * lax.rsqrt(jnp.mean(xf * xf, axis=-1, keepdims=True) + EPS)
    return y.astype(x.dtype) * g


def causal_depthwise_conv(u, w):
    width = w.shape[0]
    s = u.shape[1]
    up = jnp.pad(u, ((0, 0), (width - 1, 0), (0, 0)))
    y = w[0] * u
    for k in range(1, width):
        y = y + w[k] * up[:, width - 1 - k: width - 1 - k + s]
    return y


def sliding_window_attention(q, k, v, sinks):
    b, s, _ = q.shape
    nb = s // BLOCK
    g = N_HEADS // N_KV_HEADS
    f32 = jnp.float32
    qb = q.astype(f32).reshape(b, nb, BLOCK, N_KV_HEADS, g, HEAD_DIM)
    kb = k.astype(f32).reshape(b, nb, BLOCK, N_KV_HEADS, HEAD_DIM)
    vb = v.astype(f32).reshape(b, nb, BLOCK, N_KV_HEADS, HEAD_DIM)
    pad = ((0, 0), (1, 0), (0, 0), (0, 0), (0, 0))
    kw = jnp.concatenate([jnp.pad(kb[:, :-1], pad), kb], axis=2)
    vw = jnp.concatenate([jnp.pad(vb[:, :-1], pad), vb], axis=2)
    scores = jnp.einsum("bnqkgd,bnskd->bnkgqs", qb, kw) * (HEAD_DIM ** -0.5)
    q_idx = jnp.arange(BLOCK)[:, None] + BLOCK
    k_idx = jnp.arange(2 * BLOCK)[None, :]
    dist = q_idx - k_idx
    in_band = (dist >= 0) & (dist < WINDOW)
    has_keys = (jnp.arange(nb) > 0)[:, None, None] | (k_idx >= BLOCK)[None]
    valid = in_band[None] & has_keys
    scores = jnp.where(valid[None, :, None, None], scores, -jnp.inf)
    sink = sinks.astype(f32).reshape(1, 1, N_KV_HEADS, g, 1, 1)
    m = jnp.maximum(scores.max(axis=-1, keepdims=True), sink)
    p = jnp.exp(scores - m)
    denom = p.sum(axis=-1, keepdims=True) + jnp.exp(sink - m)
    out = jnp.einsum("bnkgqs,bnskd->bnqkgd", p / denom, vw)
    return out.reshape(b, s, D_ATTN).astype(q.dtype)


def _linear_recurrence_combine(left, right):
    a_l, b_l = left
    a_r, b_r = right
    return a_l * a_r, a_r * b_l + b_r


def rg_lru_branch(lx, lg, conv_w, conv_b, wa, ba, wx, bx, lam):
    b, s, _ = lx.shape
    xc = causal_depthwise_conv(lx, conv_w) + conv_b
    xh = xc.reshape(b, s, LRU_HEADS, LRU_HEAD_DIM)
    r = jax.nn.sigmoid(jnp.einsum("bshi,hij->bshj", xh, wa).reshape(b, s, D_LRU) + ba)
    i = jax.nn.sigmoid(jnp.einsum("bshi,hij->bshj", xh, wx).reshape(b, s, D_LRU) + bx)
    f32 = jnp.float32
    log_a = -LRU_C * r.astype(f32) * jax.nn.softplus(-lam.astype(f32))
    a = jnp.exp(log_a)
    u = jnp.sqrt(-jnp.expm1(2.0 * log_a)) * (i * xc).astype(f32)
    _, h = lax.associative_scan(_linear_recurrence_combine, (a, u), axis=1)
    return h.astype(lx.dtype) * jax.nn.gelu(lg)


def hybrid_mixer(h, w_in, sinks, conv_w, lru_conv_w, lru_conv_b, lru_wa, lru_ba,
                 lru_wx, lru_bx, lru_lambda, mix_norm, w_out):
    sizes = (D_ATTN, D_KV, D_KV, D_CONV, D_CONV, D_CONV, D_LRU, D_LRU)
    offsets = [int(o) for o in np.cumsum(sizes)[:-1]]
    proj = h @ w_in
    q, k, v, cb, cc, cx, lx, lg = jnp.split(proj, offsets, axis=-1)
    y_attn = sliding_window_attention(q, k, v, sinks)
    y_conv = cb * causal_depthwise_conv(cc * cx, conv_w)
    y_lru = rg_lru_branch(lx, lg, lru_conv_w, lru_conv_b, lru_wa, lru_ba,
                          lru_wx, lru_bx, lru_lambda)
    g_attn, g_conv, g_lru = jnp.split(mix_norm, [D_ATTN, D_ATTN + D_CONV])
    y = jnp.concatenate([rmsnorm(y_attn, g_attn), rmsnorm(y_conv, g_conv),
                         rmsnorm(y_lru, g_lru)], axis=-1)
    return y @ w_out


def swiglu(t, wg, wu, wd):
    return (jax.nn.silu(t @ wg) * (t @ wu)) @ wd


def moe_swiglu(h, router_w, wg, wu, wd):
    b, s, d = h.shape
    t = h.reshape(b * s, d)
    logits = (t @ router_w).astype(jnp.float32)
    top_logits, top_idx = lax.top_k(logits, TOP_K)
    top_w = jax.nn.softmax(top_logits, axis=-1)
    gates = jnp.einsum("tk,tke->te", top_w,
                       jax.nn.one_hot(top_idx, N_EXPERTS, dtype=jnp.float32)).astype(h.dtype)
    out = jnp.zeros_like(t)
    for e in range(N_EXPERTS):
        out = out + gates[:, e:e + 1] * swiglu(t, wg[e], wu[e], wd[e])
    return out.reshape(b, s, d)


def setup_inputs(seed: int = 0) -> dict:
    key = jax.random.key(seed)
    ks = jax.random.split(key, 24)
    f32 = jnp.float32

    def normal(k, shape, fan_in):
        return jax.random.normal(k, shape, f32) * (fan_in ** -0.5)

    def gain(k, shape):
        return 1.0 + 0.05 * jax.random.normal(k, shape, f32)

    a0 = jax.random.uniform(ks[11], (DEPTH, D_LRU), f32, minval=0.9, maxval=0.999)
    return {
        "x": jax.random.normal(ks[0], (BATCH, SEQ, D_MODEL), f32),
        "attn_norm": gain(ks[1], (DEPTH, D_MODEL)),
        "w_in": normal(ks[2], (DEPTH, D_MODEL, D_IN), D_MODEL),
        "attn_sinks": 0.5 * jax.random.normal(ks[3], (DEPTH, N_HEADS), f32),
        "conv_w": normal(ks[4], (DEPTH, CONV_WIDTH, D_CONV), CONV_WIDTH),
        "lru_conv_w": normal(ks[5], (DEPTH, LRU_CONV_WIDTH, D_LRU), LRU_CONV_WIDTH),
        "lru_conv_b": 0.01 * jax.random.normal(ks[6], (DEPTH, D_LRU), f32),
        "lru_wa": normal(ks[7], (DEPTH, LRU_HEADS, LRU_HEAD_DIM, LRU_HEAD_DIM), LRU_HEAD_DIM),
        "lru_ba": 0.1 * jax.random.normal(ks[8], (DEPTH, D_LRU), f32),
        "lru_wx": normal(ks[9], (DEPTH, LRU_HEADS, LRU_HEAD_DIM, LRU_HEAD_DIM), LRU_HEAD_DIM),
        "lru_bx": 0.1 * jax.random.normal(ks[10], (DEPTH, D_LRU), f32),
        "lru_lambda": jnp.log(a0) - jnp.log1p(-a0),
        "mix_norm": gain(ks[12], (DEPTH, D_MIX)),
        "w_out": normal(ks[13], (DEPTH, D_MIX, D_MODEL), D_MIX),
        "ffn_norm": gain(ks[14], (DEPTH, D_MODEL)),
        "dense_w_gate": normal(ks[15], (N_DENSE, D_MODEL, D_FF), D_MODEL),
        "dense_w_up": normal(ks[16], (N_DENSE, D_MODEL, D_FF), D_MODEL),
        "dense_w_down": normal(ks[17], (N_DENSE, D_FF, D_MODEL), D_FF),
        "router_w": normal(ks[18], (N_MOE, D_MODEL, N_EXPERTS), D_MODEL),
        "expert_w_gate": normal(ks[19], (N_MOE, N_EXPERTS, D_MODEL, D_EXPERT), D_MODEL),
        "expert_w_up": normal(ks[20], (N_MOE, N_EXPERTS, D_MODEL, D_EXPERT), D_MODEL),
        "expert_w_down": normal(ks[21], (N_MOE, N_EXPERTS, D_EXPERT, D_MODEL), D_EXPERT),
        "final_norm": gain(ks[22], (D_MODEL,)),
    }


def reference(x, attn_norm, w_in, attn_sinks, conv_w, lru_conv_w, lru_conv_b, lru_wa,
              lru_ba, lru_wx, lru_bx, lru_lambda, mix_norm, w_out, ffn_norm,
              dense_w_gate, dense_w_up, dense_w_down, router_w, expert_w_gate,
              expert_w_up, expert_w_down, final_norm):
    for layer in range(DEPTH):
        h = rmsnorm(x, attn_norm[layer])
        x = x + hybrid_mixer(h, w_in[layer], attn_sinks[layer], conv_w[layer],
                             lru_conv_w[layer], lru_conv_b[layer], lru_wa[layer],
                             lru_ba[layer], lru_wx[layer], lru_bx[layer],
                             lru_lambda[layer], mix_norm[layer], w_out[layer])
        h = rmsnorm(x, ffn_norm[layer])
        j = layer // 2
        if layer % 2 == 0:
            x = x + swiglu(h, dense_w_gate[j], dense_w_up[j], dense_w_down[j])
        else:
            x = x + moe_swiglu(h, router_w[j], expert_w_gate[j], expert_w_up[j],
                               expert_w_down[j])
    return rmsnorm(x, final_norm)
```

```python
import functools

import jax
import jax.numpy as jnp
from jax import lax
from jax.experimental import pallas as pl
from jax.experimental.pallas import tpu as pltpu

F32 = jnp.float32
BF16 = jnp.bfloat16

D_MODEL = 2048
HEAD_DIM = 64
D_ATTN = 1024
N_HEADS = 16
N_KV_HEADS = 2
GROUP = N_HEADS // N_KV_HEADS
BLOCK = 128
D_KV = N_KV_HEADS * HEAD_DIM
D_CONV = 512
D_LRU = 512
LRU_C = 8.0
D_IN = 3840
N_EXPERTS = 8
EPS = 1e-6
LANES = 128
SUBLANES = 8

O_Q, O_K, O_V = 0, 1024, 1152
O_CB, O_CC, O_CX = 1280, 1792, 2304
O_LX, O_LG = 2816, 3328

MIB = 2 ** 20


def _params(sem, vmem_mib):
    return pltpu.CompilerParams(dimension_semantics=sem, vmem_limit_bytes=vmem_mib * MIB)


def _rms(x, g):
    return x * lax.rsqrt(jnp.mean(x * x, axis=-1, keepdims=True) + EPS) * g


def _prenorm_kernel(x_ref, g_ref, o_ref):
    o_ref[...] = _rms(x_ref[...], g_ref[...]).astype(o_ref.dtype)


def prenorm(x, g, *, tm=512):
    t, d = x.shape
    tm = min(tm, t)
    return pl.pallas_call(
        _prenorm_kernel,
        grid=(t // tm,),
        in_specs=[pl.BlockSpec((tm, d), lambda i: (i, 0)), pl.BlockSpec((1, d), lambda i: (0, 0))],
        out_specs=pl.BlockSpec((tm, d), lambda i: (i, 0)),
        out_shape=jax.ShapeDtypeStruct((t, d), BF16),
        compiler_params=_params(("arbitrary",), 32),
        name="prenorm",
    )(x, g.reshape(1, d))


def _inproj_kernel(h_ref, w_ref, o_ref):
    o_ref[...] = jnp.dot(h_ref[...], w_ref[...], preferred_element_type=F32)


def inproj(h, w, *, tm=1024, tn=768):
    t, d = h.shape
    n = w.shape[1]
    tm = min(tm, t)
    return pl.pallas_call(
        _inproj_kernel,
        grid=(t // tm, n // tn),
        in_specs=[pl.BlockSpec((tm, d), lambda i, j: (i, 0)), pl.BlockSpec((d, tn), lambda i, j: (0, j))],
        out_specs=pl.BlockSpec((tm, tn), lambda i, j: (i, j)),
        out_shape=jax.ShapeDtypeStruct((t, n), F32),
        compiler_params=_params(("arbitrary", "arbitrary"), 48),
        name="inproj",
    )(h, w)


def _shift_rows(u, halo, k):
    r = pltpu.roll(u, k, axis=0)
    hr = pltpu.roll(halo, k, axis=0)
    row = lax.broadcasted_iota(jnp.int32, hr.shape, 0)
    top = jnp.where(row < k, hr, r[:SUBLANES])
    return jnp.concatenate([top, r[SUBLANES:]], axis=0)


def _linear_scan(a, b):
    n = a.shape[0]
    row = lax.broadcasted_iota(jnp.int32, a.shape, 0)
    s = 1
    while s < n:
        a_sh = pltpu.roll(a, s, axis=0)
        b_sh = pltpu.roll(b, s, axis=0)
        valid = row >= s
        b = jnp.where(valid, a * b_sh, 0.0) + b
        a = jnp.where(valid, a * a_sh, a)
        s *= 2
    return a, b


def _half_variants(x):
    lane = lax.broadcasted_iota(jnp.int32, x.shape, 1)
    lo = lane < HEAD_DIM
    sw = pltpu.roll(x, HEAD_DIM, axis=1)
    z = jnp.zeros_like(x)
    return [jnp.where(lo, x, z).astype(BF16), jnp.where(lo, z, sw).astype(BF16),
            jnp.where(lo, sw, z).astype(BF16), jnp.where(lo, z, x).astype(BF16)]


def _mixer_kernel(sinks_ref, proj_ref, convw_ref, lcw_ref, lcb_ref, wa_ref, ba_ref, wx_ref, bx_ref,
                  lam_ref, gn_ref, y_ref, kprev, vprev, uhalo, lxhalo, hcarry, yatt):
    tb = pl.program_id(1)
    tt = proj_ref.shape[0]

    @pl.when(tb == 0)
    def _():
        kprev[...] = jnp.zeros_like(kprev)
        vprev[...] = jnp.zeros_like(vprev)
        uhalo[...] = jnp.zeros_like(uhalo)
        lxhalo[...] = jnp.zeros_like(lxhalo)
        hcarry[...] = jnp.zeros_like(hcarry)

    row = lax.broadcasted_iota(jnp.int32, (BLOCK, BLOCK), 0)
    col = lax.broadcasted_iota(jnp.int32, (BLOCK, BLOCK), 1)
    in_cur = col <= row
    first_bias = jnp.where(tb > 0, 0.0, -jnp.inf).astype(F32)
    for sb in range(tt // BLOCK):
        rows = slice(sb * BLOCK, (sb + 1) * BLOCK)
        kc = _half_variants(proj_ref[rows, O_K:O_K + D_KV])
        vc = _half_variants(proj_ref[rows, O_V:O_V + D_KV])
        if sb == 0:
            kp = [kprev[i] for i in range(4)]
            vp = [vprev[i] for i in range(4)]
        for pair in range(N_HEADS // 2):
            kvh = (2 * pair) // GROUP
            q2 = (proj_ref[rows, pair * LANES:(pair + 1) * LANES] * (HEAD_DIM ** -0.5)).astype(BF16)
            o2 = jnp.zeros((BLOCK, LANES), F32)
            for par in range(2):
                vi = 2 * kvh + par
                kk = jnp.concatenate([kc[vi], kp[vi]], axis=0)
                s = lax.dot_general(q2, kk, (((1,), (1,)), ((), ())), preferred_element_type=F32)
                s_prev = s[:, BLOCK:]
                if sb == 0:
                    s_prev = s_prev + first_bias
                s = jnp.where(in_cur, s[:, :BLOCK], s_prev)
                sink = sinks_ref[2 * pair + par]
                m = jnp.maximum(jnp.max(s, axis=-1, keepdims=True), sink)
                p = jnp.exp(s - m)
                den = jnp.sum(p, axis=-1, keepdims=True) + jnp.exp(sink - m)
                p = p * (1.0 / den)
                pp = jnp.concatenate([jnp.where(in_cur, p, 0.0), jnp.where(in_cur, 0.0, p)], axis=1)
                vv = jnp.concatenate([vc[vi], vp[vi]], axis=0)
                o2 = o2 + jnp.dot(pp.astype(BF16), vv, preferred_element_type=F32)
            yatt[rows, pair * LANES:(pair + 1) * LANES] = o2
        kp, vp = kc, vc
    for i in range(4):
        kprev[i] = kp[i]
        vprev[i] = vp[i]
    ya = yatt[...]
    y_ref[:, 0:D_ATTN] = _rms(ya, gn_ref[:, 0:D_ATTN]).astype(y_ref.dtype)

    u = proj_ref[:, O_CC:O_CC + D_CONV] * proj_ref[:, O_CX:O_CX + D_CONV]
    uh = uhalo[...]
    conv = (convw_ref[0:1, :] * u + convw_ref[1:2, :] * _shift_rows(u, uh, 1)
            + convw_ref[2:3, :] * _shift_rows(u, uh, 2))
    uhalo[...] = u[tt - SUBLANES:, :]
    yc = proj_ref[:, O_CB:O_CB + D_CONV] * conv
    y_ref[:, D_ATTN:D_ATTN + D_CONV] = _rms(yc, gn_ref[:, D_ATTN:D_ATTN + D_CONV]).astype(y_ref.dtype)

    lx = proj_ref[:, O_LX:O_LX + D_LRU]
    lh = lxhalo[...]
    xc = lcw_ref[0:1, :] * lx + lcb_ref[...]
    for k in range(1, 4):
        xc = xc + lcw_ref[k:k + 1, :] * _shift_rows(lx, lh, k)
    lxhalo[...] = lx[tt - SUBLANES:, :]
    xcb = xc.astype(BF16)
    r = jax.nn.sigmoid(jnp.dot(xcb, wa_ref[...], preferred_element_type=F32) + ba_ref[...])
    ig = jax.nn.sigmoid(jnp.dot(xcb, wx_ref[...], preferred_element_type=F32) + bx_ref[...])
    log_a = (-LRU_C) * r * jax.nn.softplus(-lam_ref[...])
    a = jnp.exp(log_a)
    th = jnp.tanh(log_a)
    bu = jnp.sqrt(-2.0 * th / (1.0 - th)) * (ig * xc)
    a_cum, h = _linear_scan(a, bu)
    h = a_cum * hcarry[0:1, :] + h
    hcarry[...] = jnp.broadcast_to(h[tt - 1:tt, :], hcarry.shape)
    yl = h * jax.nn.gelu(proj_ref[:, O_LG:O_LG + D_LRU])
    y_ref[:, D_ATTN + D_CONV:] = _rms(yl, gn_ref[:, D_ATTN + D_CONV:]).astype(y_ref.dtype)


def mixer(proj, sinks, conv_w, lru_conv_w, lru_conv_b, wa_bd, ba, wx_bd, bx, lam, mix_norm,
          *, batch, tt=256):
    t = proj.shape[0]
    s = t // batch
    tt = min(tt, s)
    nt = s // tt
    row_map = lambda b, i: (b * nt + i, 0)
    full = lambda shape: pl.BlockSpec(shape, lambda b, i: (0,) * len(shape))
    return pl.pallas_call(
        _mixer_kernel,
        grid=(batch, nt),
        in_specs=[
            pl.BlockSpec(memory_space=pltpu.SMEM),
            pl.BlockSpec((tt, D_IN), row_map),
            full((3, D_CONV)), full((4, D_LRU)), full((1, D_LRU)),
            full((D_LRU, D_LRU)), full((1, D_LRU)), full((D_LRU, D_LRU)), full((1, D_LRU)),
            full((1, D_LRU)), full((1, D_MODEL)),
        ],
        out_specs=pl.BlockSpec((tt, D_MODEL), row_map),
        out_shape=jax.ShapeDtypeStruct((t, D_MODEL), BF16),
        scratch_shapes=[
            pltpu.VMEM((4, BLOCK, LANES), BF16), pltpu.VMEM((4, BLOCK, LANES), BF16),
            pltpu.VMEM((SUBLANES, D_CONV), F32), pltpu.VMEM((SUBLANES, D_LRU), F32),
            pltpu.VMEM((SUBLANES, D_LRU), F32), pltpu.VMEM((tt, D_ATTN), F32),
        ],
        compiler_params=_params(("arbitrary", "arbitrary"), 48),
        name="mixer",
    )(sinks, proj, conv_w, lru_conv_w, lru_conv_b.reshape(1, -1), wa_bd, ba.reshape(1, -1),
      wx_bd, bx.reshape(1, -1), lam.reshape(1, -1), mix_norm.reshape(1, -1))


def _route(hn, rw_ref, carry):
    tm = hn.shape[0]
    h_hi = hn.astype(BF16)
    h_lo = (hn - h_hi.astype(F32)).astype(BF16)
    logits = (jnp.dot(h_hi, rw_ref[0], preferred_element_type=F32)
              + jnp.dot(h_lo, rw_ref[0], preferred_element_type=F32)
              + jnp.dot(h_hi, rw_ref[1], preferred_element_type=F32))
    lane = lax.broadcasted_iota(jnp.int32, (tm, LANES), 1).astype(F32)
    neg = -jnp.inf
    l1 = jnp.where(lane < N_EXPERTS, logits, neg)
    m1 = jnp.max(l1, axis=-1, keepdims=True)
    i1 = jnp.min(jnp.where(l1 == m1, lane, float(LANES)), axis=-1, keepdims=True)
    l2 = jnp.where(lane == i1, neg, l1)
    m2 = jnp.max(l2, axis=-1, keepdims=True)
    i2 = jnp.min(jnp.where(l2 == m2, lane, float(LANES)), axis=-1, keepdims=True)
    e21 = jnp.exp(m2 - m1)
    g1 = 1.0 / (1.0 + e21)
    g2 = e21 / (1.0 + e21)
    sel1 = lane == i1
    sel2 = lane == i2
    onehot = jnp.where(sel1 | sel2, 1.0, 0.0).astype(BF16)
    rr = lax.broadcasted_iota(jnp.int32, (tm, tm), 0)
    cc = lax.broadcasted_iota(jnp.int32, (tm, tm), 1)
    tri = jnp.where(rr >= cc, 1.0, 0.0).astype(BF16)
    counts = jnp.dot(tri, onehot, preferred_element_type=F32) + carry
    r1 = jnp.sum(jnp.where(sel1, counts, 0.0), axis=-1, keepdims=True) - 1.0
    r2 = jnp.sum(jnp.where(sel2, counts, 0.0), axis=-1, keepdims=True) - 1.0
    packed = jnp.where(lane == 0.0, i1, jnp.where(lane == 1.0, i2, jnp.where(lane == 2.0, r1,
             jnp.where(lane == 3.0, r2, jnp.where(lane == 4.0, g1, jnp.where(lane == 5.0, g2, 0.0))))))
    return packed, counts[tm - 1:tm, :]


def _outproj_kernel(y_ref, w_ref, x_ref, g_ref, xo_ref, ho_ref):
    xn = x_ref[...] + jnp.dot(y_ref[...], w_ref[...], preferred_element_type=F32)
    xo_ref[...] = xn
    ho_ref[...] = _rms(xn, g_ref[...]).astype(ho_ref.dtype)


def _outproj_route_kernel(y_ref, w_ref, x_ref, g_ref, rw_ref, xo_ref, ho_ref, route_ref, cnt_ref, carry):
    @pl.when(pl.program_id(0) == 0)
    def _():
        carry[...] = jnp.zeros_like(carry)

    xn = x_ref[...] + jnp.dot(y_ref[...], w_ref[...], preferred_element_type=F32)
    xo_ref[...] = xn
    hn = _rms(xn, g_ref[...])
    ho_ref[...] = hn.astype(ho_ref.dtype)
    packed, last = _route(hn, rw_ref, carry[0:1, :])
    route_ref[...] = packed
    carry[...] = jnp.broadcast_to(last, carry.shape)
    cnt_ref[...] = jnp.broadcast_to(last, cnt_ref.shape)


def outproj(y, w, x, g, *, tm=512):
    t, d = x.shape
    tm = min(tm, t)
    rowb = pl.BlockSpec((tm, d), lambda i: (i, 0))
    return pl.pallas_call(
        _outproj_kernel,
        grid=(t // tm,),
        in_specs=[rowb, pl.BlockSpec((d, d), lambda i: (0, 0)), rowb, pl.BlockSpec((1, d), lambda i: (0, 0))],
        out_specs=[rowb, rowb],
        out_shape=[jax.ShapeDtypeStruct((t, d), F32), jax.ShapeDtypeStruct((t, d), BF16)],
        compiler_params=_params(("arbitrary",), 56),
        name="outproj",
    )(y, w, x, g.reshape(1, d))


def outproj_route(y, w, x, g, rw2, *, tm=512):
    t, d = x.shape
    tm = min(tm, t)
    rowb = pl.BlockSpec((tm, d), lambda i: (i, 0))
    return pl.pallas_call(
        _outproj_route_kernel,
        grid=(t // tm,),
        in_specs=[rowb, pl.BlockSpec((d, d), lambda i: (0, 0)), rowb, pl.BlockSpec((1, d), lambda i: (0, 0)),
                  pl.BlockSpec((2, d, LANES), lambda i: (0, 0, 0))],
        out_specs=[rowb, rowb, pl.BlockSpec((tm, LANES), lambda i: (i, 0)),
                   pl.BlockSpec((SUBLANES, LANES), lambda i: (0, 0))],
        out_shape=[jax.ShapeDtypeStruct((t, d), F32), jax.ShapeDtypeStruct((t, d), F32),
                   jax.ShapeDtypeStruct((t, LANES), F32), jax.ShapeDtypeStruct((SUBLANES, LANES), F32)],
        scratch_shapes=[pltpu.VMEM((SUBLANES, LANES), F32)],
        compiler_params=_params(("arbitrary",), 56),
        name="outproj_route",
    )(y, w, x, g.reshape(1, d), rw2)


def _swiglu_step(xb, wg_ref, wu_ref, wd_ref, acc_ref):
    g = jnp.dot(xb, wg_ref[...], preferred_element_type=F32)
    u = jnp.dot(xb, wu_ref[...], preferred_element_type=F32)
    a = (g * jax.nn.sigmoid(g) * u).astype(BF16)
    acc_ref[...] += jnp.dot(a, wd_ref[...], preferred_element_type=F32)


def _ffn_kernel(h_ref, wg_ref, wu_ref, wd_ref, x_ref, g_ref, xo_ref, ho_ref, acc_ref):
    f = pl.program_id(1)

    @pl.when(f == 0)
    def _():
        acc_ref[...] = jnp.zeros_like(acc_ref)

    _swiglu_step(h_ref[...], wg_ref, wu_ref, wd_ref, acc_ref)

    @pl.when(f == pl.num_programs(1) - 1)
    def _():
        xn = x_ref[...] + acc_ref[...]
        xo_ref[...] = xn
        ho_ref[...] = _rms(xn, g_ref[...]).astype(ho_ref.dtype)


def ffn_dense(h, wg, wu, wd, x, g, *, tm=512, tf=512):
    t, d = x.shape
    ff = wg.shape[1]
    tm = min(tm, t)
    rowb = pl.BlockSpec((tm, d), lambda i, f: (i, 0))
    return pl.pallas_call(
        _ffn_kernel,
        grid=(t // tm, ff // tf),
        in_specs=[rowb, pl.BlockSpec((d, tf), lambda i, f: (0, f)), pl.BlockSpec((d, tf), lambda i, f: (0, f)),
                  pl.BlockSpec((tf, d), lambda i, f: (f, 0)), rowb, pl.BlockSpec((1, d), lambda i, f: (0, 0))],
        out_specs=[rowb, rowb],
        out_shape=[jax.ShapeDtypeStruct((t, d), F32), jax.ShapeDtypeStruct((t, d), BF16)],
        scratch_shapes=[pltpu.VMEM((tm, d), F32)],
        compiler_params=_params(("arbitrary", "arbitrary"), 56),
        name="ffn_dense",
    )(h, wg, wu, wd, x, g.reshape(1, d))


def _expert_kernel(te_ref, tb_ref, nv_ref, xs_ref, wg_ref, wu_ref, wd_ref, ys_ref, xb_ref, acc_ref):
    i = pl.program_id(0)
    f = pl.program_id(1)
    active = nv_ref[i] > 0
    last = f == pl.num_programs(1) - 1

    @pl.when(active)
    def _():
        @pl.when(f == 0)
        def _():
            xb_ref[...] = xs_ref[...].astype(BF16)
            acc_ref[...] = jnp.zeros_like(acc_ref)

        _swiglu_step(xb_ref[...], wg_ref, wu_ref, wd_ref, acc_ref)

        @pl.when(last)
        def _():
            ys_ref[...] = acc_ref[...]

    @pl.when(jnp.logical_not(active) & last)
    def _():
        ys_ref[...] = jnp.zeros_like(ys_ref)


def moe_experts(xs, wg, wu, wd, tile_expert, tile_block, tile_rows, *, tm, tf=256):
    slots, d = xs.shape
    fe = wg.shape[2]
    nf = fe // tf
    nt = slots // tm

    def fidx(i, f, nv):
        return jnp.where(nv[i] > 0, f, nf - 1)

    grid_spec = pltpu.PrefetchScalarGridSpec(
        num_scalar_prefetch=3,
        grid=(nt, nf),
        in_specs=[
            pl.BlockSpec((tm, d), lambda i, f, te, tb, nv: (tb[i], 0)),
            pl.BlockSpec((None, d, tf), lambda i, f, te, tb, nv: (te[i], 0, fidx(i, f, nv))),
            pl.BlockSpec((None, d, tf), lambda i, f, te, tb, nv: (te[i], 0, fidx(i, f, nv))),
            pl.BlockSpec((None, tf, d), lambda i, f, te, tb, nv: (te[i], fidx(i, f, nv), 0)),
        ],
        out_specs=pl.BlockSpec((tm, d), lambda i, f, te, tb, nv: (i, 0)),
        scratch_shapes=[pltpu.VMEM((tm, d), BF16), pltpu.VMEM((tm, d), F32)],
    )
    return pl.pallas_call(
        _expert_kernel,
        grid_spec=grid_spec,
        out_shape=jax.ShapeDtypeStruct((slots, d), F32),
        compiler_params=_params(("arbitrary", "arbitrary"), 56),
        name="moe_experts",
    )(tile_expert, tile_block, tile_rows, xs, wg, wu, wd)


def _row_copy(src_hbm, src_row, dst, dst_row, sem):
    return pltpu.make_async_copy(src_hbm.at[pl.ds(src_row, 1)], dst.at[pl.ds(dst_row, 1)], sem)


def _dispatch_kernel(e_ref, r_ref, off_ref, trows_ref, h_hbm, xs_hbm, zbuf, sem, zsem):
    tmd = e_ref.shape[0] // 2
    base = pl.program_id(0) * tmd
    tm = zbuf.shape[0]

    @pl.when(pl.program_id(0) == 0)
    def _():
        zbuf[...] = jnp.zeros_like(zbuf)

        def tile_fill(j):
            return pltpu.make_async_copy(zbuf, xs_hbm.at[pl.ds(j * tm, tm)], zsem)

        def fill(j, c):
            @pl.when(trows_ref[j] < tm)
            def _():
                tile_fill(j).start()
            return c

        lax.fori_loop(0, trows_ref.shape[0], fill, 0)

        def fill_wait(j, c):
            @pl.when(trows_ref[j] < tm)
            def _():
                tile_fill(j).wait()
            return c

        lax.fori_loop(0, trows_ref.shape[0], fill_wait, 0)

    def issue(t, c):
        for k in range(2):
            slot = off_ref[e_ref[2 * t + k]] + r_ref[2 * t + k]
            _row_copy(h_hbm, base + t, xs_hbm, slot, sem).start()
        return c

    lax.fori_loop(0, tmd, issue, 0)

    def drain(t, c):
        for k in range(2):
            _row_copy(h_hbm, 0, xs_hbm, 0, sem).wait()
        return c

    lax.fori_loop(0, tmd, drain, 0)


def moe_dispatch(hn, e_flat, r_flat, off, tile_rows, *, tm, tmd=512):
    t, d = hn.shape
    tmd = min(tmd, t)
    return pl.pallas_call(
        _dispatch_kernel,
        grid=(t // tmd,),
        in_specs=[pl.BlockSpec((2 * tmd,), lambda i: (i,), memory_space=pltpu.SMEM),
                  pl.BlockSpec((2 * tmd,), lambda i: (i,), memory_space=pltpu.SMEM),
                  pl.BlockSpec(memory_space=pltpu.SMEM),
                  pl.BlockSpec(memory_space=pltpu.SMEM),
                  pl.BlockSpec(memory_space=pl.ANY)],
        out_specs=pl.BlockSpec(memory_space=pl.ANY),
        out_shape=jax.ShapeDtypeStruct((tile_rows.shape[0] * tm, d), hn.dtype),
        scratch_shapes=[pltpu.VMEM((tm, d), hn.dtype), pltpu.SemaphoreType.DMA(()),
                        pltpu.SemaphoreType.DMA(())],
        compiler_params=_params(("arbitrary",), 32),
        name="moe_dispatch",
    )(e_flat, r_flat, off, tile_rows, hn)


def _combine_kernel(e_ref, r_ref, off_ref, route_ref, x_ref, g_ref, ys_hbm, o_ref, buf, sem):
    tmc = x_ref.shape[0]

    def issue(t, c):
        for k in range(2):
            slot = off_ref[e_ref[2 * t + k]] + r_ref[2 * t + k]
            _row_copy(ys_hbm, slot, buf.at[k], t, sem).start()
        return c

    lax.fori_loop(0, tmc, issue, 0)

    def drain(t, c):
        for k in range(2):
            _row_copy(ys_hbm, 0, buf.at[k], 0, sem).wait()
        return c

    lax.fori_loop(0, tmc, drain, 0)
    g1 = route_ref[:, 4:5]
    g2 = route_ref[:, 5:6]
    xn = x_ref[...] + (g1 * buf[0] + g2 * buf[1])
    o_ref[...] = _rms(xn, g_ref[...]).astype(o_ref.dtype)


def moe_combine(ys, e_flat, r_flat, off, route, x, g, *, tmc=512):
    t, d = x.shape
    tmc = min(tmc, t)
    rowb = pl.BlockSpec((tmc, d), lambda i: (i, 0))
    return pl.pallas_call(
        _combine_kernel,
        grid=(t // tmc,),
        in_specs=[pl.BlockSpec((2 * tmc,), lambda i: (i,), memory_space=pltpu.SMEM),
                  pl.BlockSpec((2 * tmc,), lambda i: (i,), memory_space=pltpu.SMEM),
                  pl.BlockSpec(memory_space=pltpu.SMEM),
                  pl.BlockSpec((tmc, LANES), lambda i: (i, 0)),
                  rowb, pl.BlockSpec((1, d), lambda i: (0, 0)),
                  pl.BlockSpec(memory_space=pl.ANY)],
        out_specs=rowb,
        out_shape=jax.ShapeDtypeStruct((t, d), F32),
        scratch_shapes=[pltpu.VMEM((2, tmc, d), F32), pltpu.SemaphoreType.DMA(())],
        compiler_params=_params(("arbitrary",), 48),
        name="moe_combine",
    )(e_flat, r_flat, off, route, x, g.reshape(1, d), ys)


def _tile_schedule(counts, tm, nt):
    cnt = counts.astype(jnp.int32)
    ntile = (cnt + tm - 1) // tm
    tend = jnp.cumsum(ntile)
    tstart = tend - ntile
    total = tend[-1]
    tid = jnp.arange(nt, dtype=jnp.int32)
    active = tid < total
    tclamp = jnp.minimum(tid, total - 1)
    te = jnp.sum((tclamp[:, None] >= tend[None, :]).astype(jnp.int32), axis=1)
    rows = jnp.clip(cnt[te] - (tclamp - tstart[te]) * tm, 0, tm)
    return tstart * tm, te, tclamp, jnp.where(active, rows, 0)


def moe_block(hn, route, counts, x, g, wg, wu, wd, *, tm=512):
    t, d = hn.shape
    nt = (2 * t) // tm + N_EXPERTS
    off, te, tblk, trows = _tile_schedule(counts[0, :N_EXPERTS], tm, nt)
    e_flat = route[:, 0:2].astype(jnp.int32).reshape(-1)
    r_flat = route[:, 2:4].astype(jnp.int32).reshape(-1)
    xs = moe_dispatch(hn, e_flat, r_flat, off, trows, tm=tm)
    ys = moe_experts(xs, wg, wu, wd, te, tblk, trows, tm=tm)
    return moe_combine(ys, e_flat, r_flat, off, route, x, g)


def _block_diag(w):
    h, n, _ = w.shape
    eye = jnp.eye(h, dtype=w.dtype)
    return (eye[:, None, :, None] * w[:, :, None, :]).reshape(h * n, h * n)


def kernel(x, attn_norm, w_in, attn_sinks, conv_w, lru_conv_w, lru_conv_b, lru_wa, lru_ba, lru_wx,
           lru_bx, lru_lambda, mix_norm, w_out, ffn_norm, dense_w_gate, dense_w_up, dense_w_down,
           router_w, expert_w_gate, expert_w_up, expert_w_down, final_norm):
    b, s, d = x.shape
    depth = w_in.shape[0]
    assert depth == 2 and d == D_MODEL
    xf = x.reshape(b * s, d)
    h = prenorm(xf, attn_norm[0])
    out = None
    for layer in range(depth):
        proj = inproj(h, w_in[layer].astype(BF16))
        y = mixer(proj, attn_sinks[layer], conv_w[layer], lru_conv_w[layer], lru_conv_b[layer],
                  _block_diag(lru_wa[layer]).astype(BF16), lru_ba[layer],
                  _block_diag(lru_wx[layer]).astype(BF16), lru_bx[layer], lru_lambda[layer],
                  mix_norm[layer], batch=b)
        j = layer // 2
        g_next = attn_norm[layer + 1] if layer + 1 < depth else final_norm
        if layer % 2 == 0:
            xf, hf = outproj(y, w_out[layer].astype(BF16), xf, ffn_norm[layer])
            xf, h = ffn_dense(hf, dense_w_gate[j].astype(BF16), dense_w_up[j].astype(BF16),
                              dense_w_down[j].astype(BF16), xf, g_next)
        else:
            rw = jnp.pad(router_w[j], ((0, 0), (0, LANES - N_EXPERTS)))
            rw_hi = rw.astype(BF16)
            rw_lo = (rw - rw_hi.astype(F32)).astype(BF16)
            xf, hn, route, counts = outproj_route(y, w_out[layer].astype(BF16), xf, ffn_norm[layer],
                                                  jnp.stack([rw_hi, rw_lo]))
            out = moe_block(hn, route, counts, xf, g_next, expert_w_gate[j].astype(BF16),
                            expert_w_up[j].astype(BF16), expert_w_down[j].astype(BF16))
    return out.reshape(b, s, d)
```

```python
import functools

import jax
import jax.numpy as jnp
from jax import lax
from jax.experimental import pallas as pl
from jax.experimental.pallas import tpu as pltpu

F32 = jnp.float32
BF16 = jnp.bfloat16

D_MODEL = 2048
HEAD_DIM = 64
D_ATTN = 1024
N_HEADS = 16
N_KV_HEADS = 2
GROUP = N_HEADS // N_KV_HEADS
BLOCK = 128
D_KV = N_KV_HEADS * HEAD_DIM
D_CONV = 512
D_LRU = 512
LRU_C = 8.0
LOG2E = 1.4426950408889634
D_IN = 3840
N_EXPERTS = 8
EPS = 1e-6
LANES = 128
SUBLANES = 8

O_Q, O_K, O_V = 0, 1024, 1152
O_CB, O_CC, O_CX = 1280, 1792, 2304
O_LX, O_LG = 2816, 3328

MIB = 2 ** 20


def _params(sem, vmem_mib):
    return pltpu.CompilerParams(dimension_semantics=sem, vmem_limit_bytes=vmem_mib * MIB)


def _rms(x, g):
    return x * lax.rsqrt(jnp.mean(x * x, axis=-1, keepdims=True) + EPS) * g


def _prenorm_kernel(x_ref, g_ref, o_ref):
    o_ref[...] = _rms(x_ref[...], g_ref[...]).astype(o_ref.dtype)


def prenorm(x, g, *, tm=512):
    t, d = x.shape
    tm = min(tm, t)
    return pl.pallas_call(
        _prenorm_kernel,
        grid=(t // tm,),
        in_specs=[pl.BlockSpec((tm, d), lambda i: (i, 0)), pl.BlockSpec((1, d), lambda i: (0, 0))],
        out_specs=pl.BlockSpec((tm, d), lambda i: (i, 0)),
        out_shape=jax.ShapeDtypeStruct((t, d), BF16),
        compiler_params=_params(("arbitrary",), 32),
        name="prenorm",
    )(x, g.reshape(1, d))


def _inproj_kernel(h_ref, w_ref, o_ref):
    o_ref[...] = jnp.dot(h_ref[...], w_ref[...], preferred_element_type=F32)


def inproj(h, w, *, tm=1024, tn=768):
    t, d = h.shape
    n = w.shape[1]
    tm = min(tm, t)
    return pl.pallas_call(
        _inproj_kernel,
        grid=(t // tm, n // tn),
        in_specs=[pl.BlockSpec((tm, d), lambda i, j: (i, 0)), pl.BlockSpec((d, tn), lambda i, j: (0, j))],
        out_specs=pl.BlockSpec((tm, tn), lambda i, j: (i, j)),
        out_shape=jax.ShapeDtypeStruct((t, n), F32),
        compiler_params=_params(("arbitrary", "arbitrary"), 48),
        name="inproj",
    )(h, w)


def _shift_rows(u, halo, k):
    r = pltpu.roll(u, k, axis=0)
    hr = pltpu.roll(halo, k, axis=0)
    row = lax.broadcasted_iota(jnp.int32, hr.shape, 0)
    top = jnp.where(row < k, hr, r[:SUBLANES])
    return jnp.concatenate([top, r[SUBLANES:]], axis=0)


def _linear_scan(a, b, h0):
    n, c = a.shape
    g = n // SUBLANES
    a = a.reshape(g, SUBLANES, c)
    b = b.reshape(g, SUBLANES, c)
    sub = lax.broadcasted_iota(jnp.int32, a.shape, 1)
    s = 1
    while s < SUBLANES:
        a_sh = pltpu.roll(a, s, axis=1)
        b_sh = pltpu.roll(b, s, axis=1)
        valid = sub >= s
        b = jnp.where(valid, a * b_sh, 0.0) + b
        a = jnp.where(valid, a * a_sh, a)
        s *= 2
    hs = []
    carry = h0
    for i in range(g):
        h = a[i] * carry + b[i]
        carry = h[SUBLANES - 1:SUBLANES, :]
        hs.append(h)
    return jnp.concatenate(hs, axis=0), carry


def _half_variants(x):
    lane = lax.broadcasted_iota(jnp.int32, x.shape, 1)
    lo = lane < HEAD_DIM
    sw = pltpu.roll(x, HEAD_DIM, axis=1)
    z = jnp.zeros_like(x)
    return [jnp.where(lo, x, z).astype(BF16), jnp.where(lo, z, sw).astype(BF16),
            jnp.where(lo, sw, z).astype(BF16), jnp.where(lo, z, x).astype(BF16)]


def _mixer_kernel(sinks_ref, proj_ref, convw_ref, lcw_ref, lcb_ref, wa_ref, ba_ref, wx_ref, bx_ref,
                  lam_ref, gn_ref, y_ref, kprev, vprev, uhalo, lxhalo, hcarry, yatt):
    tb = pl.program_id(1)
    tt = proj_ref.shape[0]

    @pl.when(tb == 0)
    def _():
        kprev[...] = jnp.zeros_like(kprev)
        vprev[...] = jnp.zeros_like(vprev)
        uhalo[...] = jnp.zeros_like(uhalo)
        lxhalo[...] = jnp.zeros_like(lxhalo)
        hcarry[...] = jnp.zeros_like(hcarry)

    rows4 = GROUP // 2 * BLOCK
    row = lax.broadcasted_iota(jnp.int32, (rows4, LANES), 0)
    col = lax.broadcasted_iota(jnp.int32, (rows4, LANES), 1)
    in_cur = col <= (row & (BLOCK - 1))
    even = col < HEAD_DIM
    even1 = lax.broadcasted_iota(jnp.int32, (BLOCK, LANES), 1) < HEAD_DIM
    ones_even = jnp.where(even1, 1.0, 0.0).astype(BF16)
    ones_odd = jnp.where(even1, 0.0, 1.0).astype(BF16)
    den_cols = jnp.concatenate([ones_even, ones_even, ones_odd, ones_odd], axis=0)
    first_bias = jnp.where(tb > 0, 0.0, -jnp.inf).astype(F32)
    for sb in range(tt // BLOCK):
        rows = slice(sb * BLOCK, (sb + 1) * BLOCK)
        kc = _half_variants(proj_ref[rows, O_K:O_K + D_KV])
        vc = _half_variants(proj_ref[rows, O_V:O_V + D_KV])
        if sb == 0:
            kp = [kprev[i] for i in range(4)]
            vp = [vprev[i] for i in range(4)]
        for kvh in range(N_KV_HEADS):
            pairs = range(kvh * GROUP // 2, (kvh + 1) * GROUP // 2)
            lo, hi = 2 * kvh, 2 * kvh + 1
            q4 = jnp.concatenate([proj_ref[rows, p * LANES:(p + 1) * LANES] for p in pairs], axis=0)
            q4 = (q4 * (HEAD_DIM ** -0.5 * LOG2E)).astype(BF16)
            kk = jnp.concatenate([kc[lo], kp[lo], kc[hi], kp[hi]], axis=0)
            s = lax.dot_general(q4, kk, (((1,), (1,)), ((), ())), preferred_element_type=F32)
            pps, sink_terms = [], []
            for par in range(2):
                c0 = 2 * BLOCK * par
                s_prev = s[:, c0 + BLOCK:c0 + 2 * BLOCK]
                if sb == 0:
                    s_prev = s_prev + first_bias
                ssel = jnp.where(in_cur, s[:, c0:c0 + BLOCK], s_prev)
                sk = [sinks_ref[2 * p + par] * LOG2E for p in pairs]
                sink = jnp.where(row < BLOCK, sk[0], jnp.where(row < 2 * BLOCK, sk[1],
                                 jnp.where(row < 3 * BLOCK, sk[2], sk[3])))
                m = jnp.maximum(jnp.max(ssel, axis=-1, keepdims=True), sink)
                p2 = jnp.exp2(ssel - m)
                pps += [jnp.where(in_cur, p2, 0.0).astype(BF16), jnp.where(in_cur, 0.0, p2).astype(BF16)]
                sink_terms.append(jnp.exp2(sink - m))
            vv = jnp.concatenate([vc[lo], vp[lo], vc[hi], vp[hi]], axis=0)
            r = jnp.dot(jnp.concatenate(pps, axis=1), jnp.concatenate([vv, den_cols], axis=1),
                        preferred_element_type=F32)
            den = r[:, LANES:] + jnp.where(even, sink_terms[0], sink_terms[1])
            o4 = r[:, :LANES] / den
            for idx, p in enumerate(pairs):
                yatt[rows, p * LANES:(p + 1) * LANES] = o4[idx * BLOCK:(idx + 1) * BLOCK]
        kp, vp = kc, vc
    for i in range(4):
        kprev[i] = kp[i]
        vprev[i] = vp[i]
    ya = yatt[...]
    y_ref[:, 0:D_ATTN] = _rms(ya, gn_ref[:, 0:D_ATTN]).astype(y_ref.dtype)

    u = proj_ref[:, O_CC:O_CC + D_CONV] * proj_ref[:, O_CX:O_CX + D_CONV]
    uh = uhalo[...]
    conv = (convw_ref[0:1, :] * u + convw_ref[1:2, :] * _shift_rows(u, uh, 1)
            + convw_ref[2:3, :] * _shift_rows(u, uh, 2))
    uhalo[...] = u[tt - SUBLANES:, :]
    yc = proj_ref[:, O_CB:O_CB + D_CONV] * conv
    y_ref[:, D_ATTN:D_ATTN + D_CONV] = _rms(yc, gn_ref[:, D_ATTN:D_ATTN + D_CONV]).astype(y_ref.dtype)

    lx = proj_ref[:, O_LX:O_LX + D_LRU]
    lh = lxhalo[...]
    xc = lcw_ref[0:1, :] * lx + lcb_ref[...]
    for k in range(1, 4):
        xc = xc + lcw_ref[k:k + 1, :] * _shift_rows(lx, lh, k)
    lxhalo[...] = lx[tt - SUBLANES:, :]
    xcb = xc.astype(BF16)
    r = jax.nn.sigmoid(jnp.dot(xcb, wa_ref[...], preferred_element_type=F32) + ba_ref[...])
    ig = jax.nn.sigmoid(jnp.dot(xcb, wx_ref[...], preferred_element_type=F32) + bx_ref[...])
    log_a = (-LRU_C) * r * jax.nn.softplus(-lam_ref[...])
    a = jnp.exp(log_a)
    th = jnp.tanh(log_a)
    bu = jnp.sqrt(-2.0 * th / (1.0 - th)) * (ig * xc)
    h, h_last = _linear_scan(a, bu, hcarry[0:1, :])
    hcarry[...] = jnp.broadcast_to(h_last, hcarry.shape)
    yl = h * jax.nn.gelu(proj_ref[:, O_LG:O_LG + D_LRU])
    y_ref[:, D_ATTN + D_CONV:] = _rms(yl, gn_ref[:, D_ATTN + D_CONV:]).astype(y_ref.dtype)


def mixer(proj, sinks, conv_w, lru_conv_w, lru_conv_b, wa_bd, ba, wx_bd, bx, lam, mix_norm,
          *, batch, tt=256):
    t = proj.shape[0]
    s = t // batch
    tt = min(tt, s)
    nt = s // tt
    row_map = lambda b, i: (b * nt + i, 0)
    full = lambda shape: pl.BlockSpec(shape, lambda b, i: (0,) * len(shape))
    return pl.pallas_call(
        _mixer_kernel,
        grid=(batch, nt),
        in_specs=[
            pl.BlockSpec(memory_space=pltpu.SMEM),
            pl.BlockSpec((tt, D_IN), row_map),
            full((3, D_CONV)), full((4, D_LRU)), full((1, D_LRU)),
            full((D_LRU, D_LRU)), full((1, D_LRU)), full((D_LRU, D_LRU)), full((1, D_LRU)),
            full((1, D_LRU)), full((1, D_MODEL)),
        ],
        out_specs=pl.BlockSpec((tt, D_MODEL), row_map),
        out_shape=jax.ShapeDtypeStruct((t, D_MODEL), BF16),
        scratch_shapes=[
            pltpu.VMEM((4, BLOCK, LANES), BF16), pltpu.VMEM((4, BLOCK, LANES), BF16),
            pltpu.VMEM((SUBLANES, D_CONV), F32), pltpu.VMEM((SUBLANES, D_LRU), F32),
            pltpu.VMEM((SUBLANES, D_LRU), F32), pltpu.VMEM((tt, D_ATTN), F32),
        ],
        compiler_params=_params(("arbitrary", "arbitrary"), 48),
        name="mixer",
    )(sinks, proj, conv_w, lru_conv_w, lru_conv_b.reshape(1, -1), wa_bd, ba.reshape(1, -1),
      wx_bd, bx.reshape(1, -1), lam.reshape(1, -1), mix_norm.reshape(1, -1))


def _route(hn, rw_ref, carry):
    tm = hn.shape[0]
    h_hi = hn.astype(BF16)
    h_lo = (hn - h_hi.astype(F32)).astype(BF16)
    logits = (jnp.dot(h_hi, rw_ref[0], preferred_element_type=F32)
              + jnp.dot(h_lo, rw_ref[0], preferred_element_type=F32)
              + jnp.dot(h_hi, rw_ref[1], preferred_element_type=F32))
    lane = lax.broadcasted_iota(jnp.int32, (tm, LANES), 1).astype(F32)
    neg = -jnp.inf
    l1 = jnp.where(lane < N_EXPERTS, logits, neg)
    m1 = jnp.max(l1, axis=-1, keepdims=True)
    i1 = jnp.min(jnp.where(l1 == m1, lane, float(LANES)), axis=-1, keepdims=True)
    l2 = jnp.where(lane == i1, neg, l1)
    m2 = jnp.max(l2, axis=-1, keepdims=True)
    i2 = jnp.min(jnp.where(l2 == m2, lane, float(LANES)), axis=-1, keepdims=True)
    e21 = jnp.exp(m2 - m1)
    g1 = 1.0 / (1.0 + e21)
    g2 = e21 / (1.0 + e21)
    sel1 = lane == i1
    sel2 = lane == i2
    onehot = jnp.where(sel1 | sel2, 1.0, 0.0).astype(BF16)
    rr = lax.broadcasted_iota(jnp.int32, (tm, tm), 0)
    cc = lax.broadcasted_iota(jnp.int32, (tm, tm), 1)
    tri = jnp.where(rr >= cc, 1.0, 0.0).astype(BF16)
    counts = jnp.dot(tri, onehot, preferred_element_type=F32) + carry
    r1 = jnp.sum(jnp.where(sel1, counts, 0.0), axis=-1, keepdims=True) - 1.0
    r2 = jnp.sum(jnp.where(sel2, counts, 0.0), axis=-1, keepdims=True) - 1.0
    packed = jnp.where(lane == 0.0, i1, jnp.where(lane == 1.0, i2, jnp.where(lane == 2.0, r1,
             jnp.where(lane == 3.0, r2, jnp.where(lane == 4.0, g1, jnp.where(lane == 5.0, g2, 0.0))))))
    return packed, counts[tm - 1:tm, :]


def _outproj_kernel(y_ref, w_ref, x_ref, g_ref, xo_ref, ho_ref):
    xn = x_ref[...] + jnp.dot(y_ref[...], w_ref[...], preferred_element_type=F32)
    xo_ref[...] = xn
    ho_ref[...] = _rms(xn, g_ref[...]).astype(ho_ref.dtype)


def _outproj_route_kernel(y_ref, w_ref, x_ref, g_ref, rw_ref, xo_ref, ho_ref, route_ref, cnt_ref, carry):
    @pl.when(pl.program_id(0) == 0)
    def _():
        carry[...] = jnp.zeros_like(carry)

    xn = x_ref[...] + jnp.dot(y_ref[...], w_ref[...], preferred_element_type=F32)
    xo_ref[...] = xn
    hn = _rms(xn, g_ref[...])
    ho_ref[...] = hn.astype(ho_ref.dtype)
    packed, last = _route(hn, rw_ref, carry[0:1, :])
    route_ref[...] = packed
    carry[...] = jnp.broadcast_to(last, carry.shape)
    cnt_ref[...] = jnp.broadcast_to(last, cnt_ref.shape)


def outproj(y, w, x, g, *, tm=512):
    t, d = x.shape
    tm = min(tm, t)
    rowb = pl.BlockSpec((tm, d), lambda i: (i, 0))
    return pl.pallas_call(
        _outproj_kernel,
        grid=(t // tm,),
        in_specs=[rowb, pl.BlockSpec((d, d), lambda i: (0, 0)), rowb, pl.BlockSpec((1, d), lambda i: (0, 0))],
        out_specs=[rowb, rowb],
        out_shape=[jax.ShapeDtypeStruct((t, d), F32), jax.ShapeDtypeStruct((t, d), BF16)],
        compiler_params=_params(("arbitrary",), 56),
        name="outproj",
    )(y, w, x, g.reshape(1, d))


def outproj_route(y, w, x, g, rw2, *, tm=512):
    t, d = x.shape
    tm = min(tm, t)
    rowb = pl.BlockSpec((tm, d), lambda i: (i, 0))
    return pl.pallas_call(
        _outproj_route_kernel,
        grid=(t // tm,),
        in_specs=[rowb, pl.BlockSpec((d, d), lambda i: (0, 0)), rowb, pl.BlockSpec((1, d), lambda i: (0, 0)),
                  pl.BlockSpec((2, d, LANES), lambda i: (0, 0, 0))],
        out_specs=[rowb, rowb, pl.BlockSpec((tm, LANES), lambda i: (i, 0)),
                   pl.BlockSpec((SUBLANES, LANES), lambda i: (0, 0))],
        out_shape=[jax.ShapeDtypeStruct((t, d), F32), jax.ShapeDtypeStruct((t, d), F32),
                   jax.ShapeDtypeStruct((t, LANES), F32), jax.ShapeDtypeStruct((SUBLANES, LANES), F32)],
        scratch_shapes=[pltpu.VMEM((SUBLANES, LANES), F32)],
        compiler_params=_params(("arbitrary",), 56),
        name="outproj_route",
    )(y, w, x, g.reshape(1, d), rw2)


def _swiglu_step(xb, wg_ref, wu_ref, wd_ref, acc_ref):
    g = jnp.dot(xb, wg_ref[...], preferred_element_type=F32)
    u = jnp.dot(xb, wu_ref[...], preferred_element_type=F32)
    a = (g * jax.nn.sigmoid(g) * u).astype(BF16)
    acc_ref[...] += jnp.dot(a, wd_ref[...], preferred_element_type=F32)


def _ffn_kernel(h_ref, wg_ref, wu_ref, wd_ref, x_ref, g_ref, xo_ref, ho_ref, acc_ref):
    f = pl.program_id(1)

    @pl.when(f == 0)
    def _():
        acc_ref[...] = jnp.zeros_like(acc_ref)

    _swiglu_step(h_ref[...], wg_ref, wu_ref, wd_ref, acc_ref)

    @pl.when(f == pl.num_programs(1) - 1)
    def _():
        xn = x_ref[...] + acc_ref[...]
        xo_ref[...] = xn
        ho_ref[...] = _rms(xn, g_ref[...]).astype(ho_ref.dtype)


def ffn_dense(h, wg, wu, wd, x, g, *, tm=512, tf=512):
    t, d = x.shape
    ff = wg.shape[1]
    tm = min(tm, t)
    rowb = pl.BlockSpec((tm, d), lambda i, f: (i, 0))
    return pl.pallas_call(
        _ffn_kernel,
        grid=(t // tm, ff // tf),
        in_specs=[rowb, pl.BlockSpec((d, tf), lambda i, f: (0, f)), pl.BlockSpec((d, tf), lambda i, f: (0, f)),
                  pl.BlockSpec((tf, d), lambda i, f: (f, 0)), rowb, pl.BlockSpec((1, d), lambda i, f: (0, 0))],
        out_specs=[rowb, rowb],
        out_shape=[jax.ShapeDtypeStruct((t, d), F32), jax.ShapeDtypeStruct((t, d), BF16)],
        scratch_shapes=[pltpu.VMEM((tm, d), F32)],
        compiler_params=_params(("arbitrary", "arbitrary"), 56),
        name="ffn_dense",
    )(h, wg, wu, wd, x, g.reshape(1, d))


def _expert_kernel(te_ref, tb_ref, nv_ref, xs_ref, wg_ref, wu_ref, wd_ref, ys_ref, xb_ref, acc_ref):
    i = pl.program_id(0)
    f = pl.program_id(1)
    active = nv_ref[i] > 0
    last = f == pl.num_programs(1) - 1

    @pl.when(active)
    def _():
        @pl.when(f == 0)
        def _():
            xb_ref[...] = xs_ref[...].astype(BF16)
            acc_ref[...] = jnp.zeros_like(acc_ref)

        _swiglu_step(xb_ref[...], wg_ref, wu_ref, wd_ref, acc_ref)

        @pl.when(last)
        def _():
            ys_ref[...] = acc_ref[...]

    @pl.when(jnp.logical_not(active) & last)
    def _():
        ys_ref[...] = jnp.zeros_like(ys_ref)


def moe_experts(xs, wg, wu, wd, tile_expert, tile_block, tile_rows, *, tm, tf=256):
    slots, d = xs.shape
    fe = wg.shape[2]
    nf = fe // tf
    nt = slots // tm

    def fidx(i, f, nv):
        return jnp.where(nv[i] > 0, f, nf - 1)

    grid_spec = pltpu.PrefetchScalarGridSpec(
        num_scalar_prefetch=3,
        grid=(nt, nf),
        in_specs=[
            pl.BlockSpec((tm, d), lambda i, f, te, tb, nv: (tb[i], 0)),
            pl.BlockSpec((None, d, tf), lambda i, f, te, tb, nv: (te[i], 0, fidx(i, f, nv))),
            pl.BlockSpec((None, d, tf), lambda i, f, te, tb, nv: (te[i], 0, fidx(i, f, nv))),
            pl.BlockSpec((None, tf, d), lambda i, f, te, tb, nv: (te[i], fidx(i, f, nv), 0)),
        ],
        out_specs=pl.BlockSpec((tm, d), lambda i, f, te, tb, nv: (i, 0)),
        scratch_shapes=[pltpu.VMEM((tm, d), BF16), pltpu.VMEM((tm, d), F32)],
    )
    return pl.pallas_call(
        _expert_kernel,
        grid_spec=grid_spec,
        out_shape=jax.ShapeDtypeStruct((slots, d), F32),
        compiler_params=_params(("arbitrary", "arbitrary"), 56),
        name="moe_experts",
    )(tile_expert, tile_block, tile_rows, xs, wg, wu, wd)


def _row_copy(src_hbm, src_row, dst, dst_row, sem):
    return pltpu.make_async_copy(src_hbm.at[pl.ds(src_row, 1)], dst.at[pl.ds(dst_row, 1)], sem)


def _dispatch_kernel(e_ref, r_ref, off_ref, trows_ref, h_ref, xs_hbm, zbuf, sem, zsem):
    tmd = e_ref.shape[0] // 2
    tm = zbuf.shape[0]

    @pl.when(pl.program_id(0) == 0)
    def _():
        zbuf[...] = jnp.zeros_like(zbuf)

        def tile_fill(j):
            return pltpu.make_async_copy(zbuf, xs_hbm.at[pl.ds(j * tm, tm)], zsem)

        def fill(j, c):
            @pl.when(trows_ref[j] < tm)
            def _():
                tile_fill(j).start()
            return c

        lax.fori_loop(0, trows_ref.shape[0], fill, 0)

        def fill_wait(j, c):
            @pl.when(trows_ref[j] < tm)
            def _():
                tile_fill(j).wait()
            return c

        lax.fori_loop(0, trows_ref.shape[0], fill_wait, 0)

    def issue(t, c):
        for k in range(2):
            slot = off_ref[e_ref[2 * t + k]] + r_ref[2 * t + k]
            _row_copy(h_ref, t, xs_hbm, slot, sem).start()
        return c

    lax.fori_loop(0, tmd, issue, 0)

    def drain(t, c):
        for k in range(2):
            _row_copy(h_ref, 0, xs_hbm, 0, sem).wait()
        return c

    lax.fori_loop(0, tmd, drain, 0)


def moe_dispatch(hn, e_flat, r_flat, off, tile_rows, *, tm, tmd=512):
    t, d = hn.shape
    tmd = min(tmd, t)
    return pl.pallas_call(
        _dispatch_kernel,
        grid=(t // tmd,),
        in_specs=[pl.BlockSpec((2 * tmd,), lambda i: (i,), memory_space=pltpu.SMEM),
                  pl.BlockSpec((2 * tmd,), lambda i: (i,), memory_space=pltpu.SMEM),
                  pl.BlockSpec(memory_space=pltpu.SMEM),
                  pl.BlockSpec(memory_space=pltpu.SMEM),
                  pl.BlockSpec((tmd, d), lambda i: (i, 0))],
        out_specs=pl.BlockSpec(memory_space=pl.ANY),
        out_shape=jax.ShapeDtypeStruct((tile_rows.shape[0] * tm, d), hn.dtype),
        scratch_shapes=[pltpu.VMEM((tm, d), hn.dtype), pltpu.SemaphoreType.DMA(()),
                        pltpu.SemaphoreType.DMA(())],
        compiler_params=_params(("arbitrary",), 32),
        name="moe_dispatch",
    )(e_flat, r_flat, off, tile_rows, hn)


def _combine_kernel(e_ref, r_ref, off_ref, route_ref, x_ref, g_ref, ys_hbm, o_ref, buf, sem):
    tmc = x_ref.shape[0]

    def issue(t, c):
        for k in range(2):
            slot = off_ref[e_ref[2 * t + k]] + r_ref[2 * t + k]
            _row_copy(ys_hbm, slot, buf.at[k], t, sem).start()
        return c

    lax.fori_loop(0, tmc, issue, 0)

    def drain(t, c):
        for k in range(2):
            _row_copy(ys_hbm, 0, buf.at[k], 0, sem).wait()
        return c

    lax.fori_loop(0, tmc, drain, 0)
    g1 = route_ref[:, 4:5]
    g2 = route_ref[:, 5:6]
    xn = x_ref[...] + (g1 * buf[0] + g2 * buf[1])
    o_ref[...] = _rms(xn, g_ref[...]).astype(o_ref.dtype)


def moe_combine(ys, e_flat, r_flat, off, route, x, g, *, tmc=512):
    t, d = x.shape
    tmc = min(tmc, t)
    rowb = pl.BlockSpec((tmc, d), lambda i: (i, 0))
    return pl.pallas_call(
        _combine_kernel,
        grid=(t // tmc,),
        in_specs=[pl.BlockSpec((2 * tmc,), lambda i: (i,), memory_space=pltpu.SMEM),
                  pl.BlockSpec((2 * tmc,), lambda i: (i,), memory_space=pltpu.SMEM),
                  pl.BlockSpec(memory_space=pltpu.SMEM),
                  pl.BlockSpec((tmc, LANES), lambda i: (i, 0)),
                  rowb, pl.BlockSpec((1, d), lambda i: (0, 0)),
                  pl.BlockSpec(memory_space=pl.ANY)],
        out_specs=rowb,
        out_shape=jax.ShapeDtypeStruct((t, d), F32),
        scratch_shapes=[pltpu.VMEM((2, tmc, d), F32), pltpu.SemaphoreType.DMA(())],
        compiler_params=_params(("arbitrary",), 48),
        name="moe_combine",
    )(e_flat, r_flat, off, route, x, g.reshape(1, d), ys)


def _tile_schedule(counts, tm, nt):
    cnt = counts.astype(jnp.int32)
    ntile = (cnt + tm - 1) // tm
    tend = jnp.cumsum(ntile)
    tstart = tend - ntile
    total = tend[-1]
    tid = jnp.arange(nt, dtype=jnp.int32)
    active = tid < total
    tclamp = jnp.minimum(tid, total - 1)
    te = jnp.sum((tclamp[:, None] >= tend[None, :]).astype(jnp.int32), axis=1)
    rows = jnp.clip(cnt[te] - (tclamp - tstart[te]) * tm, 0, tm)
    return tstart * tm, te, tclamp, jnp.where(active, rows, 0)


def moe_block(hn, route, counts, x, g, wg, wu, wd, *, tm=512):
    t, d = hn.shape
    nt = (2 * t) // tm + N_EXPERTS
    off, te, tblk, trows = _tile_schedule(counts[0, :N_EXPERTS], tm, nt)
    e_flat = route[:, 0:2].astype(jnp.int32).reshape(-1)
    r_flat = route[:, 2:4].astype(jnp.int32).reshape(-1)
    xs = moe_dispatch(hn, e_flat, r_flat, off, trows, tm=tm)
    ys = moe_experts(xs, wg, wu, wd, te, tblk, trows, tm=tm)
    return moe_combine(ys, e_flat, r_flat, off, route, x, g)


def _block_diag(w):
    h, n, _ = w.shape
    eye = jnp.eye(h, dtype=w.dtype)
    return (eye[:, None, :, None] * w[:, :, None, :]).reshape(h * n, h * n)


def kernel(x, attn_norm, w_in, attn_sinks, conv_w, lru_conv_w, lru_conv_b, lru_wa, lru_ba, lru_wx,
           lru_bx, lru_lambda, mix_norm, w_out, ffn_norm, dense_w_gate, dense_w_up, dense_w_down,
           router_w, expert_w_gate, expert_w_up, expert_w_down, final_norm):
    b, s, d = x.shape
    depth = w_in.shape[0]
    assert depth == 2 and d == D_MODEL
    xf = x.reshape(b * s, d)
    h = prenorm(xf, attn_norm[0])
    out = None
    for layer in range(depth):
        proj = inproj(h, w_in[layer].astype(BF16))
        y = mixer(proj, attn_sinks[layer], conv_w[layer], lru_conv_w[layer], lru_conv_b[layer],
                  _block_diag(lru_wa[layer]).astype(BF16), lru_ba[layer],
                  _block_diag(lru_wx[layer]).astype(BF16), lru_bx[layer], lru_lambda[layer],
                  mix_norm[layer], batch=b)
        j = layer // 2
        g_next = attn_norm[layer + 1] if layer + 1 < depth else final_norm
        if layer % 2 == 0:
            xf, hf = outproj(y, w_out[layer].astype(BF16), xf, ffn_norm[layer])
            xf, h = ffn_dense(hf, dense_w_gate[j].astype(BF16), dense_w_up[j].astype(BF16),
                              dense_w_down[j].astype(BF16), xf, g_next)
        else:
            rw = jnp.pad(router_w[j], ((0, 0), (0, LANES - N_EXPERTS)))
            rw_hi = rw.astype(BF16)
            rw_lo = (rw - rw_hi.astype(F32)).astype(BF16)
            xf, hn, route, counts = outproj_route(y, w_out[layer].astype(BF16), xf, ffn_norm[layer],
                                                  jnp.stack([rw_hi, rw_lo]))
            out = moe_block(hn, route, counts, xf, g_next, expert_w_gate[j].astype(BF16),
                            expert_w_up[j].astype(BF16), expert_w_down[j].astype(BF16))
    return out.reshape(b, s, d)
```

```python
import functools

import jax
import jax.numpy as jnp
from jax import lax
from jax.experimental import pallas as pl
from jax.experimental.pallas import tpu as pltpu

F32 = jnp.float32
BF16 = jnp.bfloat16

D_MODEL = 2048
HEAD_DIM = 64
D_ATTN = 1024
N_HEADS = 16
N_KV_HEADS = 2
GROUP = N_HEADS // N_KV_HEADS
BLOCK = 128
D_KV = N_KV_HEADS * HEAD_DIM
D_CONV = 512
D_LRU = 512
LRU_C = 8.0
LOG2E = 1.4426950408889634
D_IN = 3840
N_EXPERTS = 8
EPS = 1e-6
LANES = 128
SUBLANES = 8

O_Q, O_K, O_V = 0, 1024, 1152
O_CB, O_CC, O_CX = 1280, 1792, 2304
O_LX, O_LG = 2816, 3328

MIB = 2 ** 20


def _params(sem, vmem_mib):
    return pltpu.CompilerParams(dimension_semantics=sem, vmem_limit_bytes=vmem_mib * MIB)


def _rms(x, g):
    return x * lax.rsqrt(jnp.mean(x * x, axis=-1, keepdims=True) + EPS) * g


def _inproj_kernel(x_ref, g_ref, w_ref, o_ref, hb_ref):
    @pl.when(pl.program_id(1) == 0)
    def _():
        hb_ref[...] = _rms(x_ref[...], g_ref[...]).astype(BF16)

    o_ref[...] = jnp.dot(hb_ref[...], w_ref[...].astype(BF16), preferred_element_type=F32)


def inproj(x, g, w, *, tm=1024, tn=768):
    t, d = x.shape
    n = w.shape[1]
    tm = min(tm, t)
    return pl.pallas_call(
        _inproj_kernel,
        grid=(t // tm, n // tn),
        in_specs=[pl.BlockSpec((tm, d), lambda i, j: (i, 0)), pl.BlockSpec((1, d), lambda i, j: (0, 0)),
                  pl.BlockSpec((d, tn), lambda i, j: (0, j))],
        out_specs=pl.BlockSpec((tm, tn), lambda i, j: (i, j)),
        out_shape=jax.ShapeDtypeStruct((t, n), F32),
        scratch_shapes=[pltpu.VMEM((tm, d), BF16)],
        compiler_params=_params(("arbitrary", "arbitrary"), 56),
        name="inproj",
    )(x, g.reshape(1, d), w)


def _shift_rows(u, halo, k):
    r = pltpu.roll(u, k, axis=0)
    hr = pltpu.roll(halo, k, axis=0)
    row = lax.broadcasted_iota(jnp.int32, hr.shape, 0)
    top = jnp.where(row < k, hr, r[:SUBLANES])
    return jnp.concatenate([top, r[SUBLANES:]], axis=0)


def _linear_scan(a, b, h0):
    n, c = a.shape
    g = n // SUBLANES
    a = a.reshape(g, SUBLANES, c)
    b = b.reshape(g, SUBLANES, c)
    sub = lax.broadcasted_iota(jnp.int32, a.shape, 1)
    s = 1
    while s < SUBLANES:
        a_sh = pltpu.roll(a, s, axis=1)
        b_sh = pltpu.roll(b, s, axis=1)
        valid = sub >= s
        b = jnp.where(valid, a * b_sh, 0.0) + b
        a = jnp.where(valid, a * a_sh, a)
        s *= 2
    hs = []
    carry = h0
    for i in range(g):
        h = a[i] * carry + b[i]
        carry = h[SUBLANES - 1:SUBLANES, :]
        hs.append(h)
    return jnp.concatenate(hs, axis=0), carry


def _half_variants(x):
    lane = lax.broadcasted_iota(jnp.int32, x.shape, 1)
    lo = lane < HEAD_DIM
    sw = pltpu.roll(x, HEAD_DIM, axis=1)
    z = jnp.zeros_like(x)
    return [jnp.where(lo, x, z).astype(BF16), jnp.where(lo, z, sw).astype(BF16),
            jnp.where(lo, sw, z).astype(BF16), jnp.where(lo, z, x).astype(BF16)]


def _mixer_kernel(sinks_ref, proj_ref, convw_ref, lcw_ref, lcb_ref, wa_ref, ba_ref, wx_ref, bx_ref,
                  lam_ref, gn_ref, y_ref, kprev, vprev, uhalo, lxhalo, hcarry, yatt):
    tb = pl.program_id(1)
    tt = proj_ref.shape[0]

    @pl.when(tb == 0)
    def _():
        kprev[...] = jnp.zeros_like(kprev)
        vprev[...] = jnp.zeros_like(vprev)
        uhalo[...] = jnp.zeros_like(uhalo)
        lxhalo[...] = jnp.zeros_like(lxhalo)
        hcarry[...] = jnp.zeros_like(hcarry)

    rows4 = GROUP // 2 * BLOCK
    row = lax.broadcasted_iota(jnp.int32, (rows4, LANES), 0)
    col = lax.broadcasted_iota(jnp.int32, (rows4, LANES), 1)
    in_cur = col <= (row & (BLOCK - 1))
    even = col < HEAD_DIM
    even1 = lax.broadcasted_iota(jnp.int32, (BLOCK, LANES), 1) < HEAD_DIM
    ones_even = jnp.where(even1, 1.0, 0.0).astype(BF16)
    ones_odd = jnp.where(even1, 0.0, 1.0).astype(BF16)
    den_cols = jnp.concatenate([ones_even, ones_even, ones_odd, ones_odd], axis=0)
    first_bias = jnp.where(tb > 0, 0.0, -jnp.inf).astype(F32)
    for sb in range(tt // BLOCK):
        rows = slice(sb * BLOCK, (sb + 1) * BLOCK)
        kc = _half_variants(proj_ref[rows, O_K:O_K + D_KV])
        vc = _half_variants(proj_ref[rows, O_V:O_V + D_KV])
        if sb == 0:
            kp = [kprev[i] for i in range(4)]
            vp = [vprev[i] for i in range(4)]
        for kvh in range(N_KV_HEADS):
            pairs = range(kvh * GROUP // 2, (kvh + 1) * GROUP // 2)
            lo, hi = 2 * kvh, 2 * kvh + 1
            q4 = jnp.concatenate([proj_ref[rows, p * LANES:(p + 1) * LANES] for p in pairs], axis=0)
            q4 = (q4 * (HEAD_DIM ** -0.5 * LOG2E)).astype(BF16)
            kk = jnp.concatenate([kc[lo], kp[lo], kc[hi], kp[hi]], axis=0)
            s = lax.dot_general(q4, kk, (((1,), (1,)), ((), ())), preferred_element_type=F32)
            pps, sink_terms = [], []
            for par in range(2):
                c0 = 2 * BLOCK * par
                s_prev = s[:, c0 + BLOCK:c0 + 2 * BLOCK]
                if sb == 0:
                    s_prev = s_prev + first_bias
                ssel = jnp.where(in_cur, s[:, c0:c0 + BLOCK], s_prev)
                sk = [sinks_ref[2 * p + par] * LOG2E for p in pairs]
                sink = jnp.where(row < BLOCK, sk[0], jnp.where(row < 2 * BLOCK, sk[1],
                                 jnp.where(row < 3 * BLOCK, sk[2], sk[3])))
                m = jnp.maximum(jnp.max(ssel, axis=-1, keepdims=True), sink)
                p2 = jnp.exp2(ssel - m)
                pps += [jnp.where(in_cur, p2, 0.0).astype(BF16), jnp.where(in_cur, 0.0, p2).astype(BF16)]
                sink_terms.append(jnp.exp2(sink - m))
            vv = jnp.concatenate([vc[lo], vp[lo], vc[hi], vp[hi]], axis=0)
            r = jnp.dot(jnp.concatenate(pps, axis=1), jnp.concatenate([vv, den_cols], axis=1),
                        preferred_element_type=F32)
            den = r[:, LANES:] + jnp.where(even, sink_terms[0], sink_terms[1])
            o4 = r[:, :LANES] / den
            for idx, p in enumerate(pairs):
                yatt[rows, p * LANES:(p + 1) * LANES] = o4[idx * BLOCK:(idx + 1) * BLOCK]
        kp, vp = kc, vc
    for i in range(4):
        kprev[i] = kp[i]
        vprev[i] = vp[i]
    ya = yatt[...]
    y_ref[:, 0:D_ATTN] = _rms(ya, gn_ref[:, 0:D_ATTN]).astype(y_ref.dtype)

    u = proj_ref[:, O_CC:O_CC + D_CONV] * proj_ref[:, O_CX:O_CX + D_CONV]
    uh = uhalo[...]
    conv = (convw_ref[0:1, :] * u + convw_ref[1:2, :] * _shift_rows(u, uh, 1)
            + convw_ref[2:3, :] * _shift_rows(u, uh, 2))
    uhalo[...] = u[tt - SUBLANES:, :]
    yc = proj_ref[:, O_CB:O_CB + D_CONV] * conv
    y_ref[:, D_ATTN:D_ATTN + D_CONV] = _rms(yc, gn_ref[:, D_ATTN:D_ATTN + D_CONV]).astype(y_ref.dtype)

    lx = proj_ref[:, O_LX:O_LX + D_LRU]
    lh = lxhalo[...]
    xc = lcw_ref[0:1, :] * lx + lcb_ref[...]
    for k in range(1, 4):
        xc = xc + lcw_ref[k:k + 1, :] * _shift_rows(lx, lh, k)
    lxhalo[...] = lx[tt - SUBLANES:, :]
    xcb = xc.astype(BF16)
    r = jax.nn.sigmoid(jnp.dot(xcb, wa_ref[...], preferred_element_type=F32) + ba_ref[...])
    ig = jax.nn.sigmoid(jnp.dot(xcb, wx_ref[...], preferred_element_type=F32) + bx_ref[...])
    log_a = (-LRU_C) * r * jax.nn.softplus(-lam_ref[...])
    a = jnp.exp(log_a)
    th = jnp.tanh(log_a)
    bu = jnp.sqrt(-2.0 * th / (1.0 - th)) * (ig * xc)
    h, h_last = _linear_scan(a, bu, hcarry[0:1, :])
    hcarry[...] = jnp.broadcast_to(h_last, hcarry.shape)
    yl = h * jax.nn.gelu(proj_ref[:, O_LG:O_LG + D_LRU])
    y_ref[:, D_ATTN + D_CONV:] = _rms(yl, gn_ref[:, D_ATTN + D_CONV:]).astype(y_ref.dtype)


def mixer(proj, sinks, conv_w, lru_conv_w, lru_conv_b, wa_bd, ba, wx_bd, bx, lam, mix_norm,
          *, batch, tt=256):
    t = proj.shape[0]
    s = t // batch
    tt = min(tt, s)
    nt = s // tt
    row_map = lambda b, i: (b * nt + i, 0)
    full = lambda shape: pl.BlockSpec(shape, lambda b, i: (0,) * len(shape))
    return pl.pallas_call(
        _mixer_kernel,
        grid=(batch, nt),
        in_specs=[
            pl.BlockSpec(memory_space=pltpu.SMEM),
            pl.BlockSpec((tt, D_IN), row_map),
            full((3, D_CONV)), full((4, D_LRU)), full((1, D_LRU)),
            full((D_LRU, D_LRU)), full((1, D_LRU)), full((D_LRU, D_LRU)), full((1, D_LRU)),
            full((1, D_LRU)), full((1, D_MODEL)),
        ],
        out_specs=pl.BlockSpec((tt, D_MODEL), row_map),
        out_shape=jax.ShapeDtypeStruct((t, D_MODEL), BF16),
        scratch_shapes=[
            pltpu.VMEM((4, BLOCK, LANES), BF16), pltpu.VMEM((4, BLOCK, LANES), BF16),
            pltpu.VMEM((SUBLANES, D_CONV), F32), pltpu.VMEM((SUBLANES, D_LRU), F32),
            pltpu.VMEM((SUBLANES, D_LRU), F32), pltpu.VMEM((tt, D_ATTN), F32),
        ],
        compiler_params=_params(("arbitrary", "arbitrary"), 48),
        name="mixer",
    )(sinks, proj, conv_w, lru_conv_w, lru_conv_b.reshape(1, -1), wa_bd, ba.reshape(1, -1),
      wx_bd, bx.reshape(1, -1), lam.reshape(1, -1), mix_norm.reshape(1, -1))


def _route(hn, rw_ref, carry):
    tm = hn.shape[0]
    h_hi = hn.astype(BF16)
    h_lo = (hn - h_hi.astype(F32)).astype(BF16)
    logits = (jnp.dot(h_hi, rw_ref[0], preferred_element_type=F32)
              + jnp.dot(h_lo, rw_ref[0], preferred_element_type=F32)
              + jnp.dot(h_hi, rw_ref[1], preferred_element_type=F32))
    lane = lax.broadcasted_iota(jnp.int32, (tm, LANES), 1).astype(F32)
    neg = -jnp.inf
    l1 = jnp.where(lane < N_EXPERTS, logits, neg)
    m1 = jnp.max(l1, axis=-1, keepdims=True)
    i1 = jnp.min(jnp.where(l1 == m1, lane, float(LANES)), axis=-1, keepdims=True)
    l2 = jnp.where(lane == i1, neg, l1)
    m2 = jnp.max(l2, axis=-1, keepdims=True)
    i2 = jnp.min(jnp.where(l2 == m2, lane, float(LANES)), axis=-1, keepdims=True)
    e21 = jnp.exp(m2 - m1)
    g1 = 1.0 / (1.0 + e21)
    g2 = e21 / (1.0 + e21)
    sel1 = lane == i1
    sel2 = lane == i2
    onehot = jnp.where(sel1 | sel2, 1.0, 0.0).astype(BF16)
    rr = lax.broadcasted_iota(jnp.int32, (tm, tm), 0)
    cc = lax.broadcasted_iota(jnp.int32, (tm, tm), 1)
    tri = jnp.where(rr >= cc, 1.0, 0.0).astype(BF16)
    counts = jnp.dot(tri, onehot, preferred_element_type=F32) + carry
    r1 = jnp.sum(jnp.where(sel1, counts, 0.0), axis=-1, keepdims=True) - 1.0
    r2 = jnp.sum(jnp.where(sel2, counts, 0.0), axis=-1, keepdims=True) - 1.0
    packed = jnp.where(lane == 0.0, i1, jnp.where(lane == 1.0, i2, jnp.where(lane == 2.0, r1,
             jnp.where(lane == 3.0, r2, jnp.where(lane == 4.0, g1, jnp.where(lane == 5.0, g2, 0.0))))))
    return packed, counts[tm - 1:tm, :]


def _outproj_kernel(y_ref, w_ref, x_ref, xo_ref):
    xo_ref[...] = x_ref[...] + jnp.dot(y_ref[...], w_ref[...], preferred_element_type=F32)


def _outproj_route_kernel(y_ref, w_ref, x_ref, g_ref, rw_ref, xo_ref, ho_ref, route_ref, cnt_ref, carry):
    @pl.when(pl.program_id(0) == 0)
    def _():
        carry[...] = jnp.zeros_like(carry)

    xn = x_ref[...] + jnp.dot(y_ref[...], w_ref[...], preferred_element_type=F32)
    xo_ref[...] = xn
    hn = _rms(xn, g_ref[...])
    ho_ref[...] = hn.astype(ho_ref.dtype)
    packed, last = _route(hn, rw_ref, carry[0:1, :])
    route_ref[...] = packed
    carry[...] = jnp.broadcast_to(last, carry.shape)
    cnt_ref[...] = jnp.broadcast_to(last, cnt_ref.shape)


def outproj(y, w, x, *, tm=512):
    t, d = x.shape
    tm = min(tm, t)
    rowb = pl.BlockSpec((tm, d), lambda i: (i, 0))
    return pl.pallas_call(
        _outproj_kernel,
        grid=(t // tm,),
        in_specs=[rowb, pl.BlockSpec((d, d), lambda i: (0, 0)), rowb],
        out_specs=rowb,
        out_shape=jax.ShapeDtypeStruct((t, d), F32),
        compiler_params=_params(("arbitrary",), 48),
        name="outproj",
    )(y, w, x)


def outproj_route(y, w, x, g, rw2, *, tm=512):
    t, d = x.shape
    tm = min(tm, t)
    rowb = pl.BlockSpec((tm, d), lambda i: (i, 0))
    return pl.pallas_call(
        _outproj_route_kernel,
        grid=(t // tm,),
        in_specs=[rowb, pl.BlockSpec((d, d), lambda i: (0, 0)), rowb, pl.BlockSpec((1, d), lambda i: (0, 0)),
                  pl.BlockSpec((2, d, LANES), lambda i: (0, 0, 0))],
        out_specs=[rowb, rowb, pl.BlockSpec((tm, LANES), lambda i: (i, 0)),
                   pl.BlockSpec((SUBLANES, LANES), lambda i: (0, 0))],
        out_shape=[jax.ShapeDtypeStruct((t, d), F32), jax.ShapeDtypeStruct((t, d), F32),
                   jax.ShapeDtypeStruct((t, LANES), F32), jax.ShapeDtypeStruct((SUBLANES, LANES), F32)],
        scratch_shapes=[pltpu.VMEM((SUBLANES, LANES), F32)],
        compiler_params=_params(("arbitrary",), 56),
        name="outproj_route",
    )(y, w, x, g.reshape(1, d), rw2)


def _swiglu_step(xb, wg, wu, wd):
    g = jnp.dot(xb, wg, preferred_element_type=F32)
    u = jnp.dot(xb, wu, preferred_element_type=F32)
    a = (g * jax.nn.sigmoid(g) * u).astype(BF16)
    return jnp.dot(a, wd, preferred_element_type=F32)


def _ffn_kernel(x_ref, g_ref, wg_ref, wu_ref, wd_ref, xo_ref, hb_ref):
    @pl.when(pl.program_id(1) == 0)
    def _():
        x = x_ref[...]
        hb_ref[...] = _rms(x, g_ref[...]).astype(BF16)
        xo_ref[...] = x

    xo_ref[...] += _swiglu_step(hb_ref[...], wg_ref[...].astype(BF16), wu_ref[...].astype(BF16),
                                wd_ref[...].astype(BF16))


def ffn_dense(x, g, wg, wu, wd, *, tm=1024, tf=256):
    t, d = x.shape
    ff = wg.shape[1]
    tm = min(tm, t)
    rowb = pl.BlockSpec((tm, d), lambda i, f: (i, 0))
    return pl.pallas_call(
        _ffn_kernel,
        grid=(t // tm, ff // tf),
        in_specs=[rowb, pl.BlockSpec((1, d), lambda i, f: (0, 0)),
                  pl.BlockSpec((d, tf), lambda i, f: (0, f)), pl.BlockSpec((d, tf), lambda i, f: (0, f)),
                  pl.BlockSpec((tf, d), lambda i, f: (f, 0))],
        out_specs=rowb,
        out_shape=jax.ShapeDtypeStruct((t, d), F32),
        scratch_shapes=[pltpu.VMEM((tm, d), BF16)],
        compiler_params=_params(("arbitrary", "arbitrary"), 58),
        name="ffn_dense",
    )(x, g.reshape(1, d), wg, wu, wd)


def _row_copy(src, src_row, dst, dst_row, sem):
    return pltpu.make_async_copy(src.at[pl.ds(src_row, 1)], dst.at[pl.ds(dst_row, 1)], sem)


def _for_rows(n, fn, start=0):
    def body(r, c):
        fn(r)
        return c

    lax.fori_loop(start, n, body, 0)


M_CHUNK = 256


def _expert_kernel(te_ref, tb_ref, nv_ref, dcur_ref, dnext_ref, wg_ref, wu_ref, wd_ref, hn_hbm, ys_hbm,
                   xg, xb, acc, yo, wgb, wub, wdb, gsem, ssem):
    i = pl.program_id(0)
    f = pl.program_id(1)
    nt = pl.num_programs(0)
    nf = pl.num_programs(1)
    tm = xg.shape[0]
    t = hn_hbm.shape[0]
    nv = nv_ref[i]

    def gather(dref, n, start):
        def one(r):
            d = dref[r]
            cp = _row_copy(hn_hbm, jnp.where(d >= t, d - t, d), xg, r, gsem)
            cp.start() if start else cp.wait()
        _for_rows(n, one)

    def scatter(dref, n, start):
        def one(r):
            cp = _row_copy(yo, r, ys_hbm, dref[r] if start else 0, ssem)
            cp.start() if start else cp.wait()
        _for_rows(n, one)

    @pl.when((i == 0) & (f == 0))
    def _():
        xg[...] = jnp.zeros_like(xg)
        gather(dcur_ref, nv, True)

    @pl.when(nv > 0)
    def _():
        @pl.when(f == 0)
        def _():
            gather(dcur_ref, nv, False)
            xb[...] = xg[...].astype(BF16)
            acc[...] = jnp.zeros_like(acc)
            n_next = jnp.where(i + 1 < nt, nv_ref[jnp.minimum(i + 1, nt - 1)], 0)
            gather(dnext_ref, n_next, True)

        wgb[...] = wg_ref[...].astype(BF16)
        wub[...] = wu_ref[...].astype(BF16)
        wdb[...] = wd_ref[...].astype(BF16)
        for c in range(tm // M_CHUNK):
            @pl.when(c * M_CHUNK < nv)
            def _():
                rows = pl.ds(c * M_CHUNK, M_CHUNK)
                acc[rows, :] += _swiglu_step(xb[rows, :], wgb[...], wub[...], wdb[...])

    @pl.when(f == nf - 1)
    def _():
        @pl.when(i > 0)
        def _():
            scatter(dcur_ref, nv_ref[jnp.maximum(i - 1, 0)], False)

        @pl.when(nv > 0)
        def _():
            yo[...] = acc[...]
            scatter(dcur_ref, nv, True)

        @pl.when(i == nt - 1)
        def _():
            scatter(dcur_ref, nv, False)


def moe_experts(hn, dst, wg, wu, wd, tile_expert, tile_block, tile_rows, *, tm, tf=256):
    t, d = hn.shape
    fe = wg.shape[2]
    nf = fe // tf
    nt = tile_rows.shape[0]

    def fidx(i, f, nv):
        return jnp.where(nv[i] > 0, f, nf - 1)

    grid_spec = pltpu.PrefetchScalarGridSpec(
        num_scalar_prefetch=3,
        grid=(nt, nf),
        in_specs=[
            pl.BlockSpec((tm,), lambda i, f, te, tb, nv: (tb[i],), memory_space=pltpu.SMEM),
            pl.BlockSpec((tm,), lambda i, f, te, tb, nv: (tb[jnp.minimum(i + 1, nt - 1)],),
                         memory_space=pltpu.SMEM),
            pl.BlockSpec((None, d, tf), lambda i, f, te, tb, nv: (te[i], 0, fidx(i, f, nv))),
            pl.BlockSpec((None, d, tf), lambda i, f, te, tb, nv: (te[i], 0, fidx(i, f, nv))),
            pl.BlockSpec((None, tf, d), lambda i, f, te, tb, nv: (te[i], fidx(i, f, nv), 0)),
            pl.BlockSpec(memory_space=pl.ANY),
        ],
        out_specs=pl.BlockSpec(memory_space=pl.ANY),
        scratch_shapes=[pltpu.VMEM((tm, d), F32), pltpu.VMEM((tm, d), BF16), pltpu.VMEM((tm, d), F32),
                        pltpu.VMEM((tm, d), F32), pltpu.VMEM((d, tf), BF16), pltpu.VMEM((d, tf), BF16),
                        pltpu.VMEM((tf, d), BF16), pltpu.SemaphoreType.DMA(()), pltpu.SemaphoreType.DMA(())],
    )
    return pl.pallas_call(
        _expert_kernel,
        grid_spec=grid_spec,
        out_shape=jax.ShapeDtypeStruct((2 * t, d), F32),
        compiler_params=_params(("arbitrary", "arbitrary"), 56),
        name="moe_experts",
    )(tile_expert, tile_block, tile_rows, dst, dst, wg, wu, wd, hn)


def _slotmap_kernel(e_ref, r_ref, off_ref, trows_ref, dst_ref):
    i = pl.program_id(0)
    tms = e_ref.shape[0] // 2
    t = tms * pl.num_programs(0)
    nt = trows_ref.shape[0]
    tm = dst_ref.shape[0] // nt

    @pl.when(i == 0)
    def _():
        def pad_tile(j):
            def one(r):
                dst_ref[j * tm + r] = 0
            _for_rows(tm, one, start=trows_ref[j])
        _for_rows(nt, pad_tile)

    def one(tok):
        for k in range(2):
            dst_ref[off_ref[e_ref[2 * tok + k]] + r_ref[2 * tok + k]] = k * t + i * tms + tok
    _for_rows(tms, one)


def moe_slotmap(e_flat, r_flat, off, tile_rows, *, tm, tms=1024):
    t = e_flat.shape[0] // 2
    tms = min(tms, t)
    return pl.pallas_call(
        _slotmap_kernel,
        grid=(t // tms,),
        in_specs=[pl.BlockSpec((2 * tms,), lambda i: (i,), memory_space=pltpu.SMEM),
                  pl.BlockSpec((2 * tms,), lambda i: (i,), memory_space=pltpu.SMEM),
                  pl.BlockSpec(memory_space=pltpu.SMEM),
                  pl.BlockSpec(memory_space=pltpu.SMEM)],
        out_specs=pl.BlockSpec(memory_space=pltpu.SMEM),
        out_shape=jax.ShapeDtypeStruct((tile_rows.shape[0] * tm,), jnp.int32),
        compiler_params=pltpu.CompilerParams(dimension_semantics=("arbitrary",)),
        name="moe_slotmap",
    )(e_flat, r_flat, off, tile_rows)


def _combine_kernel(route_ref, x_ref, y0_ref, y1_ref, g_ref, o_ref):
    xn = x_ref[...] + (route_ref[:, 4:5] * y0_ref[...] + route_ref[:, 5:6] * y1_ref[...])
    o_ref[...] = _rms(xn, g_ref[...]).astype(o_ref.dtype)


def moe_combine(ys, route, x, g, *, tmc=512):
    t, d = x.shape
    tmc = min(tmc, t)
    nb = t // tmc
    rowb = pl.BlockSpec((tmc, d), lambda i: (i, 0))
    return pl.pallas_call(
        _combine_kernel,
        grid=(nb,),
        in_specs=[pl.BlockSpec((tmc, LANES), lambda i: (i, 0)), rowb, rowb,
                  pl.BlockSpec((tmc, d), lambda i: (i + nb, 0)), pl.BlockSpec((1, d), lambda i: (0, 0))],
        out_specs=rowb,
        out_shape=jax.ShapeDtypeStruct((t, d), F32),
        compiler_params=_params(("arbitrary",), 48),
        name="moe_combine",
    )(route, x, ys, ys, g.reshape(1, d))


def _tile_schedule(counts, tm, nt):
    cnt = counts.astype(jnp.int32)
    ntile = (cnt + tm - 1) // tm
    tend = jnp.cumsum(ntile)
    tstart = tend - ntile
    total = tend[-1]
    tid = jnp.arange(nt, dtype=jnp.int32)
    active = tid < total
    tclamp = jnp.minimum(tid, total - 1)
    te = jnp.sum((tclamp[:, None] >= tend[None, :]).astype(jnp.int32), axis=1)
    rows = jnp.clip(cnt[te] - (tclamp - tstart[te]) * tm, 0, tm)
    return tstart * tm, te, tclamp, jnp.where(active, rows, 0)


def moe_block(hn, route, counts, x, g, wg, wu, wd, *, tm=1024):
    t, d = hn.shape
    nt = -(-2 * t // tm) + N_EXPERTS
    off, te, tblk, trows = _tile_schedule(counts[0, :N_EXPERTS], tm, nt)
    e_flat = route[:, 0:2].astype(jnp.int32).reshape(-1)
    r_flat = route[:, 2:4].astype(jnp.int32).reshape(-1)
    dst = moe_slotmap(e_flat, r_flat, off, trows, tm=tm)
    ys = moe_experts(hn, dst, wg, wu, wd, te, tblk, trows, tm=tm)
    return moe_combine(ys, route, x, g)


def _block_diag(w):
    h, n, _ = w.shape
    eye = jnp.eye(h, dtype=w.dtype)
    return (eye[:, None, :, None] * w[:, :, None, :]).reshape(h * n, h * n)


def kernel(x, attn_norm, w_in, attn_sinks, conv_w, lru_conv_w, lru_conv_b, lru_wa, lru_ba, lru_wx,
           lru_bx, lru_lambda, mix_norm, w_out, ffn_norm, dense_w_gate, dense_w_up, dense_w_down,
           router_w, expert_w_gate, expert_w_up, expert_w_down, final_norm):
    b, s, d = x.shape
    depth = w_in.shape[0]
    assert depth == 2 and d == D_MODEL
    xf = x.reshape(b * s, d)
    out = None
    for layer in range(depth):
        proj = inproj(xf, attn_norm[layer], w_in[layer])
        y = mixer(proj, attn_sinks[layer], conv_w[layer], lru_conv_w[layer], lru_conv_b[layer],
                  _block_diag(lru_wa[layer]).astype(BF16), lru_ba[layer],
                  _block_diag(lru_wx[layer]).astype(BF16), lru_bx[layer], lru_lambda[layer],
                  mix_norm[layer], batch=b)
        j = layer // 2
        if layer % 2 == 0:
            xf = outproj(y, w_out[layer].astype(BF16), xf)
            xf = ffn_dense(xf, ffn_norm[layer], dense_w_gate[j], dense_w_up[j], dense_w_down[j])
        else:
            rw = jnp.pad(router_w[j], ((0, 0), (0, LANES - N_EXPERTS)))
            rw_hi = rw.astype(BF16)
            rw_lo = (rw - rw_hi.astype(F32)).astype(BF16)
            xf, hn, route, counts = outproj_route(y, w_out[layer].astype(BF16), xf, ffn_norm[layer],
                                                  jnp.stack([rw_hi, rw_lo]))
            out = moe_block(hn, route, counts, xf, final_norm, expert_w_gate[j], expert_w_up[j],
                            expert_w_down[j])
    return out.reshape(b, s, d)
```

```python
import functools

import jax
import jax.numpy as jnp
from jax import lax
from jax.experimental import pallas as pl
from jax.experimental.pallas import tpu as pltpu

F32 = jnp.float32
BF16 = jnp.bfloat16

D_MODEL = 2048
HEAD_DIM = 64
D_ATTN = 1024
N_HEADS = 16
N_KV_HEADS = 2
GROUP = N_HEADS // N_KV_HEADS
BLOCK = 128
D_KV = N_KV_HEADS * HEAD_DIM
D_CONV = 512
D_LRU = 512
LRU_C = 8.0
LOG2E = 1.4426950408889634
D_IN = 3840
N_EXPERTS = 8
EPS = 1e-6
LANES = 128
SUBLANES = 8

O_Q, O_K, O_V = 0, 1024, 1152
O_CB, O_CC, O_CX = 1280, 1792, 2304
O_LX, O_LG = 2816, 3328

MIB = 2 ** 20


def _params(sem, vmem_mib):
    return pltpu.CompilerParams(dimension_semantics=sem, vmem_limit_bytes=vmem_mib * MIB)


def _rms(x, g):
    return x * lax.rsqrt(jnp.mean(x * x, axis=-1, keepdims=True) + EPS) * g


def _inproj_kernel(x_ref, g_ref, w_ref, o_ref, hb_ref):
    @pl.when(pl.program_id(1) == 0)
    def _():
        hb_ref[...] = _rms(x_ref[...], g_ref[...]).astype(BF16)

    o_ref[...] = jnp.dot(hb_ref[...], w_ref[...], preferred_element_type=F32)


def inproj(x, g, w, *, tm=1024, tn=768):
    t, d = x.shape
    n = w.shape[1]
    tm = min(tm, t)
    return pl.pallas_call(
        _inproj_kernel,
        grid=(t // tm, n // tn),
        in_specs=[pl.BlockSpec((tm, d), lambda i, j: (i, 0)), pl.BlockSpec((1, d), lambda i, j: (0, 0)),
                  pl.BlockSpec((d, tn), lambda i, j: (0, j))],
        out_specs=pl.BlockSpec((tm, tn), lambda i, j: (i, j)),
        out_shape=jax.ShapeDtypeStruct((t, n), F32),
        scratch_shapes=[pltpu.VMEM((tm, d), BF16)],
        compiler_params=_params(("arbitrary", "arbitrary"), 56),
        name="inproj",
    )(x, g.reshape(1, d), w)


def _shift_rows(u, halo, k):
    r = pltpu.roll(u, k, axis=0)
    hr = pltpu.roll(halo, k, axis=0)
    row = lax.broadcasted_iota(jnp.int32, hr.shape, 0)
    top = jnp.where(row < k, hr, r[:SUBLANES])
    return jnp.concatenate([top, r[SUBLANES:]], axis=0)


def _linear_scan(a, b, h0):
    n, c = a.shape
    g = n // SUBLANES
    a = a.reshape(g, SUBLANES, c)
    b = b.reshape(g, SUBLANES, c)
    sub = lax.broadcasted_iota(jnp.int32, a.shape, 1)
    s = 1
    while s < SUBLANES:
        a_sh = pltpu.roll(a, s, axis=1)
        b_sh = pltpu.roll(b, s, axis=1)
        valid = sub >= s
        b = jnp.where(valid, a * b_sh, 0.0) + b
        a = jnp.where(valid, a * a_sh, a)
        s *= 2
    hs = []
    carry = h0
    for i in range(g):
        h = a[i] * carry + b[i]
        carry = h[SUBLANES - 1:SUBLANES, :]
        hs.append(h)
    return jnp.concatenate(hs, axis=0), carry


def _half_variants(x):
    lane = lax.broadcasted_iota(jnp.int32, x.shape, 1)
    lo = lane < HEAD_DIM
    sw = pltpu.roll(x, HEAD_DIM, axis=1)
    z = jnp.zeros_like(x)
    return [jnp.where(lo, x, z).astype(BF16), jnp.where(lo, z, sw).astype(BF16),
            jnp.where(lo, sw, z).astype(BF16), jnp.where(lo, z, x).astype(BF16)]


def _mixer_kernel(sinks_ref, proj_ref, convw_ref, lcw_ref, lcb_ref, wa_ref, ba_ref, wx_ref, bx_ref,
                  lam_ref, gn_ref, y_ref, kprev, vprev, uhalo, lxhalo, hcarry, yatt):
    tb = pl.program_id(1)
    tt = proj_ref.shape[0]

    @pl.when(tb == 0)
    def _():
        kprev[...] = jnp.zeros_like(kprev)
        vprev[...] = jnp.zeros_like(vprev)
        uhalo[...] = jnp.zeros_like(uhalo)
        lxhalo[...] = jnp.zeros_like(lxhalo)
        hcarry[...] = jnp.zeros_like(hcarry)

    rows4 = GROUP // 2 * BLOCK
    row = lax.broadcasted_iota(jnp.int32, (rows4, LANES), 0)
    col = lax.broadcasted_iota(jnp.int32, (rows4, LANES), 1)
    in_cur = col <= (row & (BLOCK - 1))
    even = col < HEAD_DIM
    even1 = lax.broadcasted_iota(jnp.int32, (BLOCK, LANES), 1) < HEAD_DIM
    ones_even = jnp.where(even1, 1.0, 0.0).astype(BF16)
    ones_odd = jnp.where(even1, 0.0, 1.0).astype(BF16)
    den_cols = jnp.concatenate([ones_even, ones_even, ones_odd, ones_odd], axis=0)
    first_bias = jnp.where(tb > 0, 0.0, -jnp.inf).astype(F32)
    for sb in range(tt // BLOCK):
        rows = slice(sb * BLOCK, (sb + 1) * BLOCK)
        kc = _half_variants(proj_ref[rows, O_K:O_K + D_KV])
        vc = _half_variants(proj_ref[rows, O_V:O_V + D_KV])
        if sb == 0:
            kp = [kprev[i] for i in range(4)]
            vp = [vprev[i] for i in range(4)]
        for kvh in range(N_KV_HEADS):
            pairs = range(kvh * GROUP // 2, (kvh + 1) * GROUP // 2)
            lo, hi = 2 * kvh, 2 * kvh + 1
            q4 = jnp.concatenate([proj_ref[rows, p * LANES:(p + 1) * LANES] for p in pairs], axis=0)
            q4 = (q4 * (HEAD_DIM ** -0.5 * LOG2E)).astype(BF16)
            kk = jnp.concatenate([kc[lo], kp[lo], kc[hi], kp[hi]], axis=0)
            s = lax.dot_general(q4, kk, (((1,), (1,)), ((), ())), preferred_element_type=F32)
            pps, sink_terms = [], []
            for par in range(2):
                c0 = 2 * BLOCK * par
                s_prev = s[:, c0 + BLOCK:c0 + 2 * BLOCK]
                if sb == 0:
                    s_prev = s_prev + first_bias
                ssel = jnp.where(in_cur, s[:, c0:c0 + BLOCK], s_prev)
                sk = [sinks_ref[2 * p + par] * LOG2E for p in pairs]
                sink = jnp.where(row < BLOCK, sk[0], jnp.where(row < 2 * BLOCK, sk[1],
                                 jnp.where(row < 3 * BLOCK, sk[2], sk[3])))
                m = jnp.maximum(jnp.max(ssel, axis=-1, keepdims=True), sink)
                p2 = jnp.exp2(ssel - m)
                pps += [jnp.where(in_cur, p2, 0.0).astype(BF16), jnp.where(in_cur, 0.0, p2).astype(BF16)]
                sink_terms.append(jnp.exp2(sink - m))
            vv = jnp.concatenate([vc[lo], vp[lo], vc[hi], vp[hi]], axis=0)
            r = jnp.dot(jnp.concatenate(pps, axis=1), jnp.concatenate([vv, den_cols], axis=1),
                        preferred_element_type=F32)
            den = r[:, LANES:] + jnp.where(even, sink_terms[0], sink_terms[1])
            o4 = r[:, :LANES] / den
            for idx, p in enumerate(pairs):
                yatt[rows, p * LANES:(p + 1) * LANES] = o4[idx * BLOCK:(idx + 1) * BLOCK]
        kp, vp = kc, vc
    for i in range(4):
        kprev[i] = kp[i]
        vprev[i] = vp[i]
    ya = yatt[...]
    y_ref[:, 0:D_ATTN] = _rms(ya, gn_ref[:, 0:D_ATTN]).astype(y_ref.dtype)

    u = proj_ref[:, O_CC:O_CC + D_CONV] * proj_ref[:, O_CX:O_CX + D_CONV]
    uh = uhalo[...]
    conv = (convw_ref[0:1, :] * u + convw_ref[1:2, :] * _shift_rows(u, uh, 1)
            + convw_ref[2:3, :] * _shift_rows(u, uh, 2))
    uhalo[...] = u[tt - SUBLANES:, :]
    yc = proj_ref[:, O_CB:O_CB + D_CONV] * conv
    y_ref[:, D_ATTN:D_ATTN + D_CONV] = _rms(yc, gn_ref[:, D_ATTN:D_ATTN + D_CONV]).astype(y_ref.dtype)

    lx = proj_ref[:, O_LX:O_LX + D_LRU]
    lh = lxhalo[...]
    xc = lcw_ref[0:1, :] * lx + lcb_ref[...]
    for k in range(1, 4):
        xc = xc + lcw_ref[k:k + 1, :] * _shift_rows(lx, lh, k)
    lxhalo[...] = lx[tt - SUBLANES:, :]
    xcb = xc.astype(BF16)
    r = jax.nn.sigmoid(jnp.dot(xcb, wa_ref[...], preferred_element_type=F32) + ba_ref[...])
    ig = jax.nn.sigmoid(jnp.dot(xcb, wx_ref[...], preferred_element_type=F32) + bx_ref[...])
    log_a = (-LRU_C) * r * jax.nn.softplus(-lam_ref[...])
    a = jnp.exp(log_a)
    th = jnp.tanh(log_a)
    bu = jnp.sqrt(-2.0 * th / (1.0 - th)) * (ig * xc)
    h, h_last = _linear_scan(a, bu, hcarry[0:1, :])
    hcarry[...] = jnp.broadcast_to(h_last, hcarry.shape)
    yl = h * jax.nn.gelu(proj_ref[:, O_LG:O_LG + D_LRU])
    y_ref[:, D_ATTN + D_CONV:] = _rms(yl, gn_ref[:, D_ATTN + D_CONV:]).astype(y_ref.dtype)


def mixer(proj, sinks, conv_w, lru_conv_w, lru_conv_b, wa_bd, ba, wx_bd, bx, lam, mix_norm,
          *, batch, tt=256):
    t = proj.shape[0]
    s = t // batch
    tt = min(tt, s)
    nt = s // tt
    row_map = lambda b, i: (b * nt + i, 0)
    full = lambda shape: pl.BlockSpec(shape, lambda b, i: (0,) * len(shape))
    return pl.pallas_call(
        _mixer_kernel,
        grid=(batch, nt),
        in_specs=[
            pl.BlockSpec(memory_space=pltpu.SMEM),
            pl.BlockSpec((tt, D_IN), row_map),
            full((3, D_CONV)), full((4, D_LRU)), full((1, D_LRU)),
            full((D_LRU, D_LRU)), full((1, D_LRU)), full((D_LRU, D_LRU)), full((1, D_LRU)),
            full((1, D_LRU)), full((1, D_MODEL)),
        ],
        out_specs=pl.BlockSpec((tt, D_MODEL), row_map),
        out_shape=jax.ShapeDtypeStruct((t, D_MODEL), BF16),
        scratch_shapes=[
            pltpu.VMEM((4, BLOCK, LANES), BF16), pltpu.VMEM((4, BLOCK, LANES), BF16),
            pltpu.VMEM((SUBLANES, D_CONV), F32), pltpu.VMEM((SUBLANES, D_LRU), F32),
            pltpu.VMEM((SUBLANES, D_LRU), F32), pltpu.VMEM((tt, D_ATTN), F32),
        ],
        compiler_params=_params(("arbitrary", "arbitrary"), 48),
        name="mixer",
    )(sinks, proj, conv_w, lru_conv_w, lru_conv_b.reshape(1, -1), wa_bd, ba.reshape(1, -1),
      wx_bd, bx.reshape(1, -1), lam.reshape(1, -1), mix_norm.reshape(1, -1))


def _route(hn, rw_ref, carry):
    tm = hn.shape[0]
    h_hi = hn.astype(BF16)
    h_lo = (hn - h_hi.astype(F32)).astype(BF16)
    logits = (jnp.dot(h_hi, rw_ref[0], preferred_element_type=F32)
              + jnp.dot(h_lo, rw_ref[0], preferred_element_type=F32)
              + jnp.dot(h_hi, rw_ref[1], preferred_element_type=F32))
    lane = lax.broadcasted_iota(jnp.int32, (tm, LANES), 1).astype(F32)
    neg = -jnp.inf
    l1 = jnp.where(lane < N_EXPERTS, logits, neg)
    m1 = jnp.max(l1, axis=-1, keepdims=True)
    i1 = jnp.min(jnp.where(l1 == m1, lane, float(LANES)), axis=-1, keepdims=True)
    l2 = jnp.where(lane == i1, neg, l1)
    m2 = jnp.max(l2, axis=-1, keepdims=True)
    i2 = jnp.min(jnp.where(l2 == m2, lane, float(LANES)), axis=-1, keepdims=True)
    e21 = jnp.exp(m2 - m1)
    g1 = 1.0 / (1.0 + e21)
    g2 = e21 / (1.0 + e21)
    sel1 = lane == i1
    sel2 = lane == i2
    onehot = jnp.where(sel1 | sel2, 1.0, 0.0).astype(BF16)
    rr = lax.broadcasted_iota(jnp.int32, (tm, tm), 0)
    cc = lax.broadcasted_iota(jnp.int32, (tm, tm), 1)
    tri = jnp.where(rr >= cc, 1.0, 0.0).astype(BF16)
    counts = jnp.dot(tri, onehot, preferred_element_type=F32) + carry
    r1 = jnp.sum(jnp.where(sel1, counts, 0.0), axis=-1, keepdims=True) - 1.0
    r2 = jnp.sum(jnp.where(sel2, counts, 0.0), axis=-1, keepdims=True) - 1.0
    packed = jnp.where(lane == 0.0, i1, jnp.where(lane == 1.0, i2, jnp.where(lane == 2.0, r1,
             jnp.where(lane == 3.0, r2, jnp.where(lane == 4.0, g1, jnp.where(lane == 5.0, g2, 0.0))))))
    return packed, counts[tm - 1:tm, :]


def _outproj_kernel(y_ref, w_ref, x_ref, xo_ref):
    xo_ref[...] = x_ref[...] + jnp.dot(y_ref[...], w_ref[...], preferred_element_type=F32)


def _outproj_route_kernel(y_ref, w_ref, x_ref, g_ref, rw_ref, xo_ref, ho_ref, route_ref, cnt_ref, carry):
    @pl.when(pl.program_id(0) == 0)
    def _():
        carry[...] = jnp.zeros_like(carry)

    xn = x_ref[...] + jnp.dot(y_ref[...], w_ref[...], preferred_element_type=F32)
    xo_ref[...] = xn
    hn = _rms(xn, g_ref[...])
    ho_ref[...] = hn.astype(ho_ref.dtype)
    packed, last = _route(hn, rw_ref, carry[0:1, :])
    route_ref[...] = packed
    carry[...] = jnp.broadcast_to(last, carry.shape)
    cnt_ref[...] = jnp.broadcast_to(last, cnt_ref.shape)


def outproj(y, w, x, *, tm=512):
    t, d = x.shape
    tm = min(tm, t)
    rowb = pl.BlockSpec((tm, d), lambda i: (i, 0))
    return pl.pallas_call(
        _outproj_kernel,
        grid=(t // tm,),
        in_specs=[rowb, pl.BlockSpec((d, d), lambda i: (0, 0)), rowb],
        out_specs=rowb,
        out_shape=jax.ShapeDtypeStruct((t, d), F32),
        compiler_params=_params(("arbitrary",), 48),
        name="outproj",
    )(y, w, x)


def outproj_route(y, w, x, g, rw2, *, tm=512):
    t, d = x.shape
    tm = min(tm, t)
    rowb = pl.BlockSpec((tm, d), lambda i: (i, 0))
    return pl.pallas_call(
        _outproj_route_kernel,
        grid=(t // tm,),
        in_specs=[rowb, pl.BlockSpec((d, d), lambda i: (0, 0)), rowb, pl.BlockSpec((1, d), lambda i: (0, 0)),
                  pl.BlockSpec((2, d, LANES), lambda i: (0, 0, 0))],
        out_specs=[rowb, rowb, pl.BlockSpec((tm, LANES), lambda i: (i, 0)),
                   pl.BlockSpec((SUBLANES, LANES), lambda i: (0, 0))],
        out_shape=[jax.ShapeDtypeStruct((t, d), F32), jax.ShapeDtypeStruct((t, d), F32),
                   jax.ShapeDtypeStruct((t, LANES), F32), jax.ShapeDtypeStruct((SUBLANES, LANES), F32)],
        scratch_shapes=[pltpu.VMEM((SUBLANES, LANES), F32)],
        compiler_params=_params(("arbitrary",), 56),
        name="outproj_route",
    )(y, w, x, g.reshape(1, d), rw2)


def _swiglu_step(xb, wg, wu, wd):
    g = jnp.dot(xb, wg, preferred_element_type=F32)
    u = jnp.dot(xb, wu, preferred_element_type=F32)
    a = (g * jax.nn.sigmoid(g) * u).astype(BF16)
    return jnp.dot(a, wd, preferred_element_type=F32)


def _ffn_kernel(x_ref, g_ref, wg_ref, wu_ref, wd_ref, xo_ref, hb_ref):
    @pl.when(pl.program_id(1) == 0)
    def _():
        x = x_ref[...]
        hb_ref[...] = _rms(x, g_ref[...]).astype(BF16)
        xo_ref[...] = x

    xo_ref[...] += _swiglu_step(hb_ref[...], wg_ref[...].astype(BF16), wu_ref[...].astype(BF16),
                                wd_ref[...].astype(BF16))


def ffn_dense(x, g, wg, wu, wd, *, tm=1024, tf=256):
    t, d = x.shape
    ff = wg.shape[1]
    tm = min(tm, t)
    rowb = pl.BlockSpec((tm, d), lambda i, f: (i, 0))
    return pl.pallas_call(
        _ffn_kernel,
        grid=(t // tm, ff // tf),
        in_specs=[rowb, pl.BlockSpec((1, d), lambda i, f: (0, 0)),
                  pl.BlockSpec((d, tf), lambda i, f: (0, f)), pl.BlockSpec((d, tf), lambda i, f: (0, f)),
                  pl.BlockSpec((tf, d), lambda i, f: (f, 0))],
        out_specs=rowb,
        out_shape=jax.ShapeDtypeStruct((t, d), F32),
        scratch_shapes=[pltpu.VMEM((tm, d), BF16)],
        compiler_params=_params(("arbitrary", "arbitrary"), 58),
        name="ffn_dense",
    )(x, g.reshape(1, d), wg, wu, wd)


def _row_copy(src, src_row, dst, dst_row, sem):
    return pltpu.make_async_copy(src.at[pl.ds(src_row, 1)], dst.at[pl.ds(dst_row, 1)], sem)


def _for_rows(n, fn, start=0):
    def body(r, c):
        fn(r)
        return c

    lax.fori_loop(start, n, body, 0)


M_ALWAYS = 512
M_CHUNK = 256


def _expert_kernel(te_ref, tb_ref, nv_ref, dprev_ref, dcur_ref, dnext_ref, wg_ref, wu_ref, wd_ref,
                   hn_hbm, ys_hbm, xg, xb, acc, yo, wgb, wub, wdb, gsem, ssem, *, nf):
    i = pl.program_id(0)
    f = pl.program_id(1)
    nt = pl.num_programs(0)
    tm = xb.shape[0]
    nb = xg.shape[0]
    rps = nb // nf
    t = hn_hbm.shape[0]
    nv = nv_ref[i]
    nv_prev = jnp.where(i > 0, nv_ref[jnp.maximum(i - 1, 0)], 0)
    last_f = f == nf - 1

    def gather_row(dref, r):
        d = dref[jnp.minimum(r, tm - 1)]
        tok = d - t * ((d >= t).astype(jnp.int32) + (d >= 2 * t).astype(jnp.int32))
        _row_copy(hn_hbm, tok, xg, r, gsem).start()

    def scatter_row(dref, tile, r, live):
        d = dref[jnp.minimum(r, tm - 1)]
        dummy = 2 * t + ((tile + 1) % 2) * nb + r
        dest = jnp.where(live & (r < tm) & (d < 2 * t), d, dummy)
        _row_copy(yo.at[tile % 2], jnp.minimum(r, tm - 1), ys_hbm, dest, ssem.at[(tile + 1) % 2]).start()

    def wait_gather():
        pltpu.make_async_copy(ys_hbm.at[pl.ds(0, nb)], xg, gsem).wait()

    def wait_scatter(tile):
        sem = ssem.at[(tile + 1) % 2]
        pltpu.make_async_copy(yo.at[0], ys_hbm.at[pl.ds(0, tm)], sem).wait()
        if nb > tm:
            pltpu.make_async_copy(yo.at[0, pl.ds(0, nb - tm)], ys_hbm.at[pl.ds(0, nb - tm)], sem).wait()

    def batch_rows(fn):
        _for_rows(nb, fn)

    @pl.when((i == 0) & (f == 0))
    def _():
        xg[...] = jnp.zeros_like(xg)
        yo[...] = jnp.zeros_like(yo)
        for start in range(0, 2 * nb, tm):
            n = min(tm, 2 * nb - start)
            cp = pltpu.make_async_copy(yo.at[0, pl.ds(0, n)], ys_hbm.at[pl.ds(2 * t + start, n)], gsem)
            cp.start()
            cp.wait()
        batch_rows(lambda r: gather_row(dcur_ref, r))

    @pl.when((f == 0) & ((i == 0) | (nv_prev > 0)))
    def _():
        wait_gather()

    @pl.when(nv > 0)
    def _():
        @pl.when(f == 0)
        def _():
            xb[...] = xg[0:tm, :].astype(BF16)
            acc[...] = jnp.zeros_like(acc)

        @pl.when(last_f & (i > 0))
        def _():
            wait_scatter(i)

        for j in range(rps):
            r = f * rps + j
            gather_row(dnext_ref, r)
            scatter_row(dprev_ref, i + 1, r, i > 0)
        wgb[...] = wg_ref[...].astype(BF16)
        wub[...] = wu_ref[...].astype(BF16)
        wdb[...] = wd_ref[...].astype(BF16)
        m0 = min(M_ALWAYS, tm)
        acc[0:m0, :] += _swiglu_step(xb[0:m0, :], wgb[...], wub[...], wdb[...])
        for c in range(m0, tm, M_CHUNK):
            @pl.when(c < nv)
            def _():
                rows = pl.ds(c, M_CHUNK)
                acc[rows, :] += _swiglu_step(xb[rows, :], wgb[...], wub[...], wdb[...])

        @pl.when(last_f)
        def _():
            yo[i % 2] = acc[...]

    @pl.when((f == 0) & (nv == 0) & (nv_prev > 0))
    def _():
        wait_scatter(i)
        batch_rows(lambda r: scatter_row(dprev_ref, i + 1, r, True))
        wait_scatter(i + 1)

    @pl.when(last_f & (i == nt - 1) & (nv > 0))
    def _():
        wait_gather()
        wait_scatter(i + 1)
        batch_rows(lambda r: scatter_row(dcur_ref, i, r, True))
        wait_scatter(i)


def moe_experts(hn, dst, wg, wu, wd, tile_expert, tile_block, tile_rows, *, tm, tf=256):
    t, d = hn.shape
    fe = wg.shape[2]
    nf = fe // tf
    nt = tile_rows.shape[0]
    rps = -(-tm // nf)
    while (rps * nf - tm) % SUBLANES:
        rps += 1
    nb = rps * nf

    def fidx(i, f, nv):
        return jnp.where(nv[i] > 0, f, nf - 1)

    grid_spec = pltpu.PrefetchScalarGridSpec(
        num_scalar_prefetch=3,
        grid=(nt, nf),
        in_specs=[
            pl.BlockSpec((tm,), lambda i, f, te, tb, nv: (tb[jnp.maximum(i - 1, 0)],),
                         memory_space=pltpu.SMEM),
            pl.BlockSpec((tm,), lambda i, f, te, tb, nv: (tb[i],), memory_space=pltpu.SMEM),
            pl.BlockSpec((tm,), lambda i, f, te, tb, nv: (tb[jnp.minimum(i + 1, nt - 1)],),
                         memory_space=pltpu.SMEM),
            pl.BlockSpec((None, d, tf), lambda i, f, te, tb, nv: (te[i], 0, fidx(i, f, nv))),
            pl.BlockSpec((None, d, tf), lambda i, f, te, tb, nv: (te[i], 0, fidx(i, f, nv))),
            pl.BlockSpec((None, tf, d), lambda i, f, te, tb, nv: (te[i], fidx(i, f, nv), 0)),
            pl.BlockSpec(memory_space=pl.ANY),
        ],
        out_specs=pl.BlockSpec(memory_space=pl.ANY),
        scratch_shapes=[pltpu.VMEM((nb, d), F32), pltpu.VMEM((tm, d), BF16), pltpu.VMEM((tm, d), F32),
                        pltpu.VMEM((2, tm, d), F32), pltpu.VMEM((d, tf), BF16), pltpu.VMEM((d, tf), BF16),
                        pltpu.VMEM((tf, d), BF16), pltpu.SemaphoreType.DMA(()),
                        pltpu.SemaphoreType.DMA((2,))],
    )
    return pl.pallas_call(
        functools.partial(_expert_kernel, nf=nf),
        grid_spec=grid_spec,
        out_shape=jax.ShapeDtypeStruct((2 * t + 2 * nb, d), F32),
        compiler_params=_params(("arbitrary", "arbitrary"), 60),
        name="moe_experts",
    )(tile_expert, tile_block, tile_rows, dst, dst, dst, wg, wu, wd, hn)


def _slotmap_kernel(e_ref, r_ref, off_ref, trows_ref, dst_ref):
    i = pl.program_id(0)
    tms = e_ref.shape[0] // 2
    t = tms * pl.num_programs(0)
    nt = trows_ref.shape[0]
    tm = dst_ref.shape[0] // nt

    @pl.when(i == 0)
    def _():
        def pad_tile(j):
            def one(r):
                dst_ref[j * tm + r] = 2 * t
            _for_rows(tm, one, start=trows_ref[j])
        _for_rows(nt, pad_tile)

    def one(tok):
        for k in range(2):
            dst_ref[off_ref[e_ref[2 * tok + k]] + r_ref[2 * tok + k]] = k * t + i * tms + tok
    _for_rows(tms, one)


def moe_slotmap(e_flat, r_flat, off, tile_rows, *, tm, tms=1024):
    t = e_flat.shape[0] // 2
    tms = min(tms, t)
    return pl.pallas_call(
        _slotmap_kernel,
        grid=(t // tms,),
        in_specs=[pl.BlockSpec((2 * tms,), lambda i: (i,), memory_space=pltpu.SMEM),
                  pl.BlockSpec((2 * tms,), lambda i: (i,), memory_space=pltpu.SMEM),
                  pl.BlockSpec(memory_space=pltpu.SMEM),
                  pl.BlockSpec(memory_space=pltpu.SMEM)],
        out_specs=pl.BlockSpec(memory_space=pltpu.SMEM),
        out_shape=jax.ShapeDtypeStruct((tile_rows.shape[0] * tm,), jnp.int32),
        compiler_params=pltpu.CompilerParams(dimension_semantics=("arbitrary",)),
        name="moe_slotmap",
    )(e_flat, r_flat, off, tile_rows)


def _combine_kernel(route_ref, x_ref, y0_ref, y1_ref, g_ref, o_ref):
    xn = x_ref[...] + (route_ref[:, 4:5] * y0_ref[...] + route_ref[:, 5:6] * y1_ref[...])
    o_ref[...] = _rms(xn, g_ref[...]).astype(o_ref.dtype)


def moe_combine(ys, route, x, g, *, tmc=512):
    t, d = x.shape
    tmc = min(tmc, t)
    nb = t // tmc
    rowb = pl.BlockSpec((tmc, d), lambda i: (i, 0))
    return pl.pallas_call(
        _combine_kernel,
        grid=(nb,),
        in_specs=[pl.BlockSpec((tmc, LANES), lambda i: (i, 0)), rowb, rowb,
                  pl.BlockSpec((tmc, d), lambda i: (i + nb, 0)), pl.BlockSpec((1, d), lambda i: (0, 0))],
        out_specs=rowb,
        out_shape=jax.ShapeDtypeStruct((t, d), F32),
        compiler_params=_params(("arbitrary",), 48),
        name="moe_combine",
    )(route, x, ys, ys, g.reshape(1, d))


def _tile_schedule(counts, tm, nt):
    cnt = counts.astype(jnp.int32)
    ntile = (cnt + tm - 1) // tm
    tend = jnp.cumsum(ntile)
    tstart = tend - ntile
    total = tend[-1]
    tid = jnp.arange(nt, dtype=jnp.int32)
    active = tid < total
    tclamp = jnp.minimum(tid, total - 1)
    te = jnp.sum((tclamp[:, None] >= tend[None, :]).astype(jnp.int32), axis=1)
    rows = jnp.clip(cnt[te] - (tclamp - tstart[te]) * tm, 0, tm)
    return tstart * tm, te, tclamp, jnp.where(active, rows, 0)


def moe_block(hn, route, counts, x, g, wg, wu, wd, *, tm=1024):
    t, d = hn.shape
    nt = -(-2 * t // tm) + N_EXPERTS
    off, te, tblk, trows = _tile_schedule(counts[0, :N_EXPERTS], tm, nt)
    e_flat = route[:, 0:2].astype(jnp.int32).reshape(-1)
    r_flat = route[:, 2:4].astype(jnp.int32).reshape(-1)
    dst = moe_slotmap(e_flat, r_flat, off, trows, tm=tm)
    ys = moe_experts(hn, dst, wg, wu, wd, te, tblk, trows, tm=tm)
    return moe_combine(ys, route, x, g)


def _block_diag(w):
    h, n, _ = w.shape
    eye = jnp.eye(h, dtype=w.dtype)
    return (eye[:, None, :, None] * w[:, :, None, :]).reshape(h * n, h * n)


def kernel(x, attn_norm, w_in, attn_sinks, conv_w, lru_conv_w, lru_conv_b, lru_wa, lru_ba, lru_wx,
           lru_bx, lru_lambda, mix_norm, w_out, ffn_norm, dense_w_gate, dense_w_up, dense_w_down,
           router_w, expert_w_gate, expert_w_up, expert_w_down, final_norm):
    b, s, d = x.shape
    depth = w_in.shape[0]
    assert depth == 2 and d == D_MODEL
    xf = x.reshape(b * s, d)
    out = None
    for layer in range(depth):
        proj = inproj(xf, attn_norm[layer], w_in[layer].astype(BF16))
        y = mixer(proj, attn_sinks[layer], conv_w[layer], lru_conv_w[layer], lru_conv_b[layer],
                  _block_diag(lru_wa[layer]).astype(BF16), lru_ba[layer],
                  _block_diag(lru_wx[layer]).astype(BF16), lru_bx[layer], lru_lambda[layer],
                  mix_norm[layer], batch=b)
        j = layer // 2
        if layer % 2 == 0:
            xf = outproj(y, w_out[layer].astype(BF16), xf)
            xf = ffn_dense(xf, ffn_norm[layer], dense_w_gate[j], dense_w_up[j], dense_w_down[j])
        else:
            rw = jnp.pad(router_w[j], ((0, 0), (0, LANES - N_EXPERTS)))
            rw_hi = rw.astype(BF16)
            rw_lo = (rw - rw_hi.astype(F32)).astype(BF16)
            xf, hn, route, counts = outproj_route(y, w_out[layer].astype(BF16), xf, ffn_norm[layer],
                                                  jnp.stack([rw_hi, rw_lo]))
            out = moe_block(hn, route, counts, xf, final_norm, expert_w_gate[j], expert_w_up[j],
                            expert_w_down[j])
    return out.reshape(b, s, d)
```

```python
import functools

import jax
import jax.numpy as jnp
from jax import lax
from jax.experimental import pallas as pl
from jax.experimental.pallas import tpu as pltpu

F32 = jnp.float32
BF16 = jnp.bfloat16

D_MODEL = 2048
HEAD_DIM = 64
D_ATTN = 1024
N_HEADS = 16
N_KV_HEADS = 2
GROUP = N_HEADS // N_KV_HEADS
BLOCK = 128
D_KV = N_KV_HEADS * HEAD_DIM
D_CONV = 512
D_LRU = 512
LRU_C = 8.0
LOG2E = 1.4426950408889634
D_IN = 3840
N_EXPERTS = 8
EPS = 1e-6
LANES = 128
SUBLANES = 8

O_Q, O_K, O_V = 0, 1024, 1152
O_CB, O_CC, O_CX = 1280, 1792, 2304
O_LX, O_LG = 2816, 3328

MIB = 2 ** 20


def _params(sem, vmem_mib):
    return pltpu.CompilerParams(dimension_semantics=sem, vmem_limit_bytes=vmem_mib * MIB)


def _rms(x, g):
    return x * lax.rsqrt(jnp.mean(x * x, axis=-1, keepdims=True) + EPS) * g


def _inproj_kernel(x_ref, g_ref, w_ref, o_ref, hb_ref):
    @pl.when(pl.program_id(1) == 0)
    def _():
        hb_ref[...] = _rms(x_ref[...], g_ref[...]).astype(BF16)

    o_ref[...] = jnp.dot(hb_ref[...], w_ref[...], preferred_element_type=F32)


def inproj(x, g, w, *, tm=1024, tn=768):
    t, d = x.shape
    n = w.shape[1]
    tm = min(tm, t)
    return pl.pallas_call(
        _inproj_kernel,
        grid=(t // tm, n // tn),
        in_specs=[pl.BlockSpec((tm, d), lambda i, j: (i, 0)), pl.BlockSpec((1, d), lambda i, j: (0, 0)),
                  pl.BlockSpec((d, tn), lambda i, j: (0, j))],
        out_specs=pl.BlockSpec((tm, tn), lambda i, j: (i, j)),
        out_shape=jax.ShapeDtypeStruct((t, n), F32),
        scratch_shapes=[pltpu.VMEM((tm, d), BF16)],
        compiler_params=_params(("arbitrary", "arbitrary"), 56),
        name="inproj",
    )(x, g.reshape(1, d), w)


def _shift_rows(u, halo, k):
    r = pltpu.roll(u, k, axis=0)
    hr = pltpu.roll(halo, k, axis=0)
    row = lax.broadcasted_iota(jnp.int32, hr.shape, 0)
    top = jnp.where(row < k, hr, r[:SUBLANES])
    return jnp.concatenate([top, r[SUBLANES:]], axis=0)


def _linear_scan(a, b, h0):
    n, c = a.shape
    g = n // SUBLANES
    a = a.reshape(g, SUBLANES, c)
    b = b.reshape(g, SUBLANES, c)
    sub = lax.broadcasted_iota(jnp.int32, a.shape, 1)
    s = 1
    while s < SUBLANES:
        a_sh = pltpu.roll(a, s, axis=1)
        b_sh = pltpu.roll(b, s, axis=1)
        valid = sub >= s
        b = jnp.where(valid, a * b_sh, 0.0) + b
        a = jnp.where(valid, a * a_sh, a)
        s *= 2
    hs = []
    carry = h0
    for i in range(g):
        h = a[i] * carry + b[i]
        carry = h[SUBLANES - 1:SUBLANES, :]
        hs.append(h)
    return jnp.concatenate(hs, axis=0), carry


def _half_variants(x):
    lane = lax.broadcasted_iota(jnp.int32, x.shape, 1)
    lo = lane < HEAD_DIM
    sw = pltpu.roll(x, HEAD_DIM, axis=1)
    z = jnp.zeros_like(x)
    return [jnp.where(lo, x, z).astype(BF16), jnp.where(lo, z, sw).astype(BF16),
            jnp.where(lo, sw, z).astype(BF16), jnp.where(lo, z, x).astype(BF16)]


def _mixer_kernel(sinks_ref, proj_ref, convw_ref, lcw_ref, lcb_ref, wa_ref, ba_ref, wx_ref, bx_ref,
                  lam_ref, gn_ref, y_ref, kprev, vprev, uhalo, lxhalo, hcarry, yatt):
    tb = pl.program_id(1)
    tt = proj_ref.shape[0]

    @pl.when(tb == 0)
    def _():
        kprev[...] = jnp.zeros_like(kprev)
        vprev[...] = jnp.zeros_like(vprev)
        uhalo[...] = jnp.zeros_like(uhalo)
        lxhalo[...] = jnp.zeros_like(lxhalo)
        hcarry[...] = jnp.zeros_like(hcarry)

    rows4 = GROUP // 2 * BLOCK
    row = lax.broadcasted_iota(jnp.int32, (rows4, LANES), 0)
    col = lax.broadcasted_iota(jnp.int32, (rows4, LANES), 1)
    in_cur = col <= (row & (BLOCK - 1))
    even = col < HEAD_DIM
    even1 = lax.broadcasted_iota(jnp.int32, (BLOCK, LANES), 1) < HEAD_DIM
    ones_even = jnp.where(even1, 1.0, 0.0).astype(BF16)
    ones_odd = jnp.where(even1, 0.0, 1.0).astype(BF16)
    den_cols = jnp.concatenate([ones_even, ones_even, ones_odd, ones_odd], axis=0)
    first_bias = jnp.where(tb > 0, 0.0, -jnp.inf).astype(F32)
    for sb in range(tt // BLOCK):
        rows = slice(sb * BLOCK, (sb + 1) * BLOCK)
        kc = _half_variants(proj_ref[rows, O_K:O_K + D_KV])
        vc = _half_variants(proj_ref[rows, O_V:O_V + D_KV])
        if sb == 0:
            kp = [kprev[i] for i in range(4)]
            vp = [vprev[i] for i in range(4)]
        for kvh in range(N_KV_HEADS):
            pairs = range(kvh * GROUP // 2, (kvh + 1) * GROUP // 2)
            lo, hi = 2 * kvh, 2 * kvh + 1
            q4 = jnp.concatenate([proj_ref[rows, p * LANES:(p + 1) * LANES] for p in pairs], axis=0)
            q4 = (q4 * (HEAD_DIM ** -0.5 * LOG2E)).astype(BF16)
            kk = jnp.concatenate([kc[lo], kp[lo], kc[hi], kp[hi]], axis=0)
            s = lax.dot_general(q4, kk, (((1,), (1,)), ((), ())), preferred_element_type=F32)
            pps, sink_terms = [], []
            for par in range(2):
                c0 = 2 * BLOCK * par
                s_prev = s[:, c0 + BLOCK:c0 + 2 * BLOCK]
                if sb == 0:
                    s_prev = s_prev + first_bias
                ssel = jnp.where(in_cur, s[:, c0:c0 + BLOCK], s_prev)
                sk = [sinks_ref[2 * p + par] * LOG2E for p in pairs]
                sink = jnp.where(row < BLOCK, sk[0], jnp.where(row < 2 * BLOCK, sk[1],
                                 jnp.where(row < 3 * BLOCK, sk[2], sk[3])))
                m = jnp.maximum(jnp.max(ssel, axis=-1, keepdims=True), sink)
                p2 = jnp.exp2(ssel - m)
                pps += [jnp.where(in_cur, p2, 0.0).astype(BF16), jnp.where(in_cur, 0.0, p2).astype(BF16)]
                sink_terms.append(jnp.exp2(sink - m))
            vv = jnp.concatenate([vc[lo], vp[lo], vc[hi], vp[hi]], axis=0)
            r = jnp.dot(jnp.concatenate(pps, axis=1), jnp.concatenate([vv, den_cols], axis=1),
                        preferred_element_type=F32)
            den = r[:, LANES:] + jnp.where(even, sink_terms[0], sink_terms[1])
            o4 = r[:, :LANES] / den
            for idx, p in enumerate(pairs):
                yatt[rows, p * LANES:(p + 1) * LANES] = o4[idx * BLOCK:(idx + 1) * BLOCK]
        kp, vp = kc, vc
    for i in range(4):
        kprev[i] = kp[i]
        vprev[i] = vp[i]
    ya = yatt[...]
    y_ref[:, 0:D_ATTN] = _rms(ya, gn_ref[:, 0:D_ATTN]).astype(y_ref.dtype)

    u = proj_ref[:, O_CC:O_CC + D_CONV] * proj_ref[:, O_CX:O_CX + D_CONV]
    uh = uhalo[...]
    conv = (convw_ref[0:1, :] * u + convw_ref[1:2, :] * _shift_rows(u, uh, 1)
            + convw_ref[2:3, :] * _shift_rows(u, uh, 2))
    uhalo[...] = u[tt - SUBLANES:, :]
    yc = proj_ref[:, O_CB:O_CB + D_CONV] * conv
    y_ref[:, D_ATTN:D_ATTN + D_CONV] = _rms(yc, gn_ref[:, D_ATTN:D_ATTN + D_CONV]).astype(y_ref.dtype)

    lx = proj_ref[:, O_LX:O_LX + D_LRU]
    lh = lxhalo[...]
    xc = lcw_ref[0:1, :] * lx + lcb_ref[...]
    for k in range(1, 4):
        xc = xc + lcw_ref[k:k + 1, :] * _shift_rows(lx, lh, k)
    lxhalo[...] = lx[tt - SUBLANES:, :]
    xcb = xc.astype(BF16)
    r = jax.nn.sigmoid(jnp.dot(xcb, wa_ref[...], preferred_element_type=F32) + ba_ref[...])
    ig = jax.nn.sigmoid(jnp.dot(xcb, wx_ref[...], preferred_element_type=F32) + bx_ref[...])
    log_a = (-LRU_C) * r * jax.nn.softplus(-lam_ref[...])
    a = jnp.exp(log_a)
    th = jnp.tanh(log_a)
    bu = jnp.sqrt(-2.0 * th / (1.0 - th)) * (ig * xc)
    h, h_last = _linear_scan(a, bu, hcarry[0:1, :])
    hcarry[...] = jnp.broadcast_to(h_last, hcarry.shape)
    yl = h * jax.nn.gelu(proj_ref[:, O_LG:O_LG + D_LRU])
    y_ref[:, D_ATTN + D_CONV:] = _rms(yl, gn_ref[:, D_ATTN + D_CONV:]).astype(y_ref.dtype)


def mixer(proj, sinks, conv_w, lru_conv_w, lru_conv_b, wa_bd, ba, wx_bd, bx, lam, mix_norm,
          *, batch, tt=256):
    t = proj.shape[0]
    s = t // batch
    tt = min(tt, s)
    nt = s // tt
    row_map = lambda b, i: (b * nt + i, 0)
    full = lambda shape: pl.BlockSpec(shape, lambda b, i: (0,) * len(shape))
    return pl.pallas_call(
        _mixer_kernel,
        grid=(batch, nt),
        in_specs=[
            pl.BlockSpec(memory_space=pltpu.SMEM),
            pl.BlockSpec((tt, D_IN), row_map),
            full((3, D_CONV)), full((4, D_LRU)), full((1, D_LRU)),
            full((D_LRU, D_LRU)), full((1, D_LRU)), full((D_LRU, D_LRU)), full((1, D_LRU)),
            full((1, D_LRU)), full((1, D_MODEL)),
        ],
        out_specs=pl.BlockSpec((tt, D_MODEL), row_map),
        out_shape=jax.ShapeDtypeStruct((t, D_MODEL), BF16),
        scratch_shapes=[
            pltpu.VMEM((4, BLOCK, LANES), BF16), pltpu.VMEM((4, BLOCK, LANES), BF16),
            pltpu.VMEM((SUBLANES, D_CONV), F32), pltpu.VMEM((SUBLANES, D_LRU), F32),
            pltpu.VMEM((SUBLANES, D_LRU), F32), pltpu.VMEM((tt, D_ATTN), F32),
        ],
        compiler_params=_params(("arbitrary", "arbitrary"), 48),
        name="mixer",
    )(sinks, proj, conv_w, lru_conv_w, lru_conv_b.reshape(1, -1), wa_bd, ba.reshape(1, -1),
      wx_bd, bx.reshape(1, -1), lam.reshape(1, -1), mix_norm.reshape(1, -1))


def _route(hn, rw_ref, carry):
    tm = hn.shape[0]
    h_hi = hn.astype(BF16)
    h_lo = (hn - h_hi.astype(F32)).astype(BF16)
    res = jnp.dot(jnp.concatenate([h_hi, h_lo], axis=1), rw_ref[...], preferred_element_type=F32)
    logits = res[:, :LANES] + res[:, LANES:]
    lane = lax.broadcasted_iota(jnp.int32, (tm, LANES), 1).astype(F32)
    neg = -jnp.inf
    l1 = jnp.where(lane < N_EXPERTS, logits, neg)
    m1 = jnp.max(l1, axis=-1, keepdims=True)
    i1 = jnp.min(jnp.where(l1 == m1, lane, float(LANES)), axis=-1, keepdims=True)
    l2 = jnp.where(lane == i1, neg, l1)
    m2 = jnp.max(l2, axis=-1, keepdims=True)
    i2 = jnp.min(jnp.where(l2 == m2, lane, float(LANES)), axis=-1, keepdims=True)
    e21 = jnp.exp(m2 - m1)
    g1 = 1.0 / (1.0 + e21)
    g2 = e21 / (1.0 + e21)
    sel1 = lane == i1
    sel2 = lane == i2
    onehot = jnp.where(sel1 | sel2, 1.0, 0.0).astype(BF16)
    rr = lax.broadcasted_iota(jnp.int32, (tm, tm), 0)
    cc = lax.broadcasted_iota(jnp.int32, (tm, tm), 1)
    tri = jnp.where(rr >= cc, 1.0, 0.0).astype(BF16)
    counts = jnp.dot(tri, onehot, preferred_element_type=F32) + carry
    r1 = jnp.sum(jnp.where(sel1, counts, 0.0), axis=-1, keepdims=True) - 1.0
    r2 = jnp.sum(jnp.where(sel2, counts, 0.0), axis=-1, keepdims=True) - 1.0
    packed = jnp.where(lane == 0.0, i1, jnp.where(lane == 1.0, i2, jnp.where(lane == 2.0, r1,
             jnp.where(lane == 3.0, r2, jnp.where(lane == 4.0, g1, jnp.where(lane == 5.0, g2, 0.0))))))
    return packed, counts[tm - 1:tm, :]


def _outproj_kernel(y_ref, w_ref, x_ref, xo_ref):
    xo_ref[...] = x_ref[...] + jnp.dot(y_ref[...], w_ref[...], preferred_element_type=F32)


def _outproj_route_kernel(y_ref, w_ref, x_ref, g_ref, rw_ref, xo_ref, ho_ref, route_ref, cnt_ref,
                          carry, hprev):
    i = pl.program_id(0)

    @pl.when(i == 0)
    def _():
        carry[...] = jnp.zeros_like(carry)
        hprev[...] = jnp.zeros_like(hprev)

    packed, last = _route(hprev[...], rw_ref, carry[0:1, :])
    route_ref[...] = packed
    counts = jnp.where(i > 0, jnp.broadcast_to(last, carry.shape), carry[...])
    carry[...] = counts
    cnt_ref[...] = counts
    xn = x_ref[...] + jnp.dot(y_ref[...], w_ref[...], preferred_element_type=F32)
    xo_ref[...] = xn
    hn = _rms(xn, g_ref[...])
    ho_ref[...] = hn
    hprev[...] = hn


def outproj(y, w, x, *, tm=512):
    t, d = x.shape
    tm = min(tm, t)
    rowb = pl.BlockSpec((tm, d), lambda i: (i, 0))
    return pl.pallas_call(
        _outproj_kernel,
        grid=(t // tm,),
        in_specs=[rowb, pl.BlockSpec((d, d), lambda i: (0, 0)), rowb],
        out_specs=rowb,
        out_shape=jax.ShapeDtypeStruct((t, d), F32),
        compiler_params=_params(("arbitrary",), 48),
        name="outproj",
    )(y, w, x)


def outproj_route(y, w, x, g, rw2, *, tm=512):
    t, d = x.shape
    tm = min(tm, t)
    n = t // tm
    rowb = pl.BlockSpec((tm, d), lambda i: (jnp.minimum(i, n - 1), 0))
    return pl.pallas_call(
        _outproj_route_kernel,
        grid=(n + 1,),
        in_specs=[rowb, pl.BlockSpec((d, d), lambda i: (0, 0)), rowb, pl.BlockSpec((1, d), lambda i: (0, 0)),
                  pl.BlockSpec((2 * d, 2 * LANES), lambda i: (0, 0))],
        out_specs=[rowb, rowb, pl.BlockSpec((tm, LANES), lambda i: (jnp.maximum(i - 1, 0), 0)),
                   pl.BlockSpec((SUBLANES, LANES), lambda i: (0, 0))],
        out_shape=[jax.ShapeDtypeStruct((t, d), F32), jax.ShapeDtypeStruct((t, d), F32),
                   jax.ShapeDtypeStruct((t, LANES), F32), jax.ShapeDtypeStruct((SUBLANES, LANES), F32)],
        scratch_shapes=[pltpu.VMEM((SUBLANES, LANES), F32), pltpu.VMEM((tm, d), F32)],
        compiler_params=_params(("arbitrary",), 56),
        name="outproj_route",
    )(y, w, x, g.reshape(1, d), rw2)


def _swiglu_step(xb, wg, wu, wd):
    g = jnp.dot(xb, wg, preferred_element_type=F32)
    u = jnp.dot(xb, wu, preferred_element_type=F32)
    a = (g * jax.nn.sigmoid(g) * u).astype(BF16)
    return jnp.dot(a, wd, preferred_element_type=F32)


def _ffn_kernel(x_ref, g_ref, wg_ref, wu_ref, wd_ref, xo_ref, hb_ref):
    @pl.when(pl.program_id(1) == 0)
    def _():
        x = x_ref[...]
        hb_ref[...] = _rms(x, g_ref[...]).astype(BF16)
        xo_ref[...] = x

    xo_ref[...] += _swiglu_step(hb_ref[...], wg_ref[...].astype(BF16), wu_ref[...].astype(BF16),
                                wd_ref[...].astype(BF16))


def ffn_dense(x, g, wg, wu, wd, *, tm=1024, tf=256):
    t, d = x.shape
    ff = wg.shape[1]
    tm = min(tm, t)
    rowb = pl.BlockSpec((tm, d), lambda i, f: (i, 0))
    return pl.pallas_call(
        _ffn_kernel,
        grid=(t // tm, ff // tf),
        in_specs=[rowb, pl.BlockSpec((1, d), lambda i, f: (0, 0)),
                  pl.BlockSpec((d, tf), lambda i, f: (0, f)), pl.BlockSpec((d, tf), lambda i, f: (0, f)),
                  pl.BlockSpec((tf, d), lambda i, f: (f, 0))],
        out_specs=rowb,
        out_shape=jax.ShapeDtypeStruct((t, d), F32),
        scratch_shapes=[pltpu.VMEM((tm, d), BF16)],
        compiler_params=_params(("arbitrary", "arbitrary"), 58),
        name="ffn_dense",
    )(x, g.reshape(1, d), wg, wu, wd)


def _row_copy(src, src_row, dst, dst_row, sem):
    return pltpu.make_async_copy(src.at[pl.ds(src_row, 1)], dst.at[pl.ds(dst_row, 1)], sem)


def _for_rows(n, fn):
    def body(r, c):
        fn(r)
        return c

    lax.fori_loop(0, n, body, 0)


M_ALWAYS = 512
M_CHUNK = 256


def _expert_kernel(te_ref, tb_ref, nv_ref, xs_ref, wg_ref, wu_ref, wd_ref, ys_ref, xb, wgb, wub, wdb):
    i = pl.program_id(0)
    f = pl.program_id(1)
    tm = xb.shape[0]
    nv = nv_ref[i]

    @pl.when(nv > 0)
    def _():
        @pl.when(f == 0)
        def _():
            xb[...] = xs_ref[...].astype(BF16)
            ys_ref[...] = jnp.zeros_like(ys_ref)

        wgb[...] = wg_ref[...].astype(BF16)
        wub[...] = wu_ref[...].astype(BF16)
        wdb[...] = wd_ref[...].astype(BF16)
        m0 = min(M_ALWAYS, tm)
        ys_ref[0:m0, :] += _swiglu_step(xb[0:m0, :], wgb[...], wub[...], wdb[...])
        for c in range(m0, tm, M_CHUNK):
            @pl.when(c < nv)
            def _():
                rows = pl.ds(c, M_CHUNK)
                ys_ref[rows, :] += _swiglu_step(xb[rows, :], wgb[...], wub[...], wdb[...])

    @pl.when((nv == 0) & (f == pl.num_programs(1) - 1))
    def _():
        ys_ref[...] = jnp.zeros_like(ys_ref)


def moe_experts(xs, wg, wu, wd, tile_expert, tile_block, tile_rows, *, tm, tf=256):
    slots, d = xs.shape
    fe = wg.shape[2]
    nf = fe // tf
    nt = slots // tm

    def fidx(i, f, nv):
        return jnp.where(nv[i] > 0, f, nf - 1)

    grid_spec = pltpu.PrefetchScalarGridSpec(
        num_scalar_prefetch=3,
        grid=(nt, nf),
        in_specs=[
            pl.BlockSpec((tm, d), lambda i, f, te, tb, nv: (tb[i], 0)),
            pl.BlockSpec((None, d, tf), lambda i, f, te, tb, nv: (te[i], 0, fidx(i, f, nv))),
            pl.BlockSpec((None, d, tf), lambda i, f, te, tb, nv: (te[i], 0, fidx(i, f, nv))),
            pl.BlockSpec((None, tf, d), lambda i, f, te, tb, nv: (te[i], fidx(i, f, nv), 0)),
        ],
        out_specs=pl.BlockSpec((tm, d), lambda i, f, te, tb, nv: (i, 0)),
        scratch_shapes=[pltpu.VMEM((tm, d), BF16), pltpu.VMEM((d, tf), BF16), pltpu.VMEM((d, tf), BF16),
                        pltpu.VMEM((tf, d), BF16)],
    )
    return pl.pallas_call(
        _expert_kernel,
        grid_spec=grid_spec,
        out_shape=jax.ShapeDtypeStruct((slots, d), F32),
        compiler_params=_params(("arbitrary", "arbitrary"), 58),
        name="moe_experts",
    )(tile_expert, tile_block, tile_rows, xs, wg, wu, wd)


def _slot(e_ref, r_ref, off_ref, j):
    return off_ref[e_ref[j]] + r_ref[j]


def _dispatch_kernel(e_ref, r_ref, off_ref, trows_ref, h_ref, xs_hbm, zbuf, sem, zsem):
    tmd = h_ref.shape[0]
    tm = zbuf.shape[0]

    @pl.when(pl.program_id(0) == 0)
    def _():
        zbuf[...] = jnp.zeros_like(zbuf)

        def tile_fill(j):
            return pltpu.make_async_copy(zbuf, xs_hbm.at[pl.ds(j * tm, tm)], zsem)

        def fill(j):
            @pl.when(trows_ref[j] < tm)
            def _():
                tile_fill(j).start()

        def fill_wait(j):
            @pl.when(trows_ref[j] < tm)
            def _():
                tile_fill(j).wait()

        _for_rows(trows_ref.shape[0], fill)
        _for_rows(trows_ref.shape[0], fill_wait)

    for tok in range(tmd):
        for k in range(2):
            _row_copy(h_ref, tok, xs_hbm, _slot(e_ref, r_ref, off_ref, 2 * tok + k), sem).start()
    for _ in range(2):
        pltpu.make_async_copy(h_ref, xs_hbm.at[pl.ds(0, tmd)], sem).wait()


def moe_dispatch(hn, e_flat, r_flat, off, tile_rows, *, tm, tmd=512):
    t, d = hn.shape
    tmd = min(tmd, t)
    return pl.pallas_call(
        _dispatch_kernel,
        grid=(t // tmd,),
        in_specs=[pl.BlockSpec((2 * tmd,), lambda i: (i,), memory_space=pltpu.SMEM),
                  pl.BlockSpec((2 * tmd,), lambda i: (i,), memory_space=pltpu.SMEM),
                  pl.BlockSpec(memory_space=pltpu.SMEM),
                  pl.BlockSpec(memory_space=pltpu.SMEM),
                  pl.BlockSpec((tmd, d), lambda i: (i, 0))],
        out_specs=pl.BlockSpec(memory_space=pl.ANY),
        out_shape=jax.ShapeDtypeStruct((tile_rows.shape[0] * tm, d), hn.dtype),
        scratch_shapes=[pltpu.VMEM((tm, d), hn.dtype), pltpu.SemaphoreType.DMA(()),
                        pltpu.SemaphoreType.DMA(())],
        compiler_params=_params(("arbitrary",), 32),
        name="moe_dispatch",
    )(e_flat, r_flat, off, tile_rows, hn)


def _combine_kernel(e_ref, r_ref, en_ref, rn_ref, off_ref, route_ref, x_ref, g_ref, ys_hbm, o_ref, buf, sem):
    i = pl.program_id(0)
    n = pl.num_programs(0)
    tmc = x_ref.shape[0]

    def gather(eref, rref, slot):
        for tok in range(tmc):
            for k in range(2):
                _row_copy(ys_hbm, _slot(eref, rref, off_ref, 2 * tok + k), buf.at[slot, k], tok,
                          sem.at[slot]).start()

    @pl.when(i == 0)
    def _():
        gather(e_ref, r_ref, 0)

    @pl.when(i + 1 < n)
    def _():
        gather(en_ref, rn_ref, (i + 1) % 2)

    cur = i % 2
    for k in range(2):
        pltpu.make_async_copy(ys_hbm.at[pl.ds(0, tmc)], buf.at[cur, k], sem.at[cur]).wait()
    xn = x_ref[...] + (route_ref[:, 4:5] * buf[cur, 0] + route_ref[:, 5:6] * buf[cur, 1])
    o_ref[...] = _rms(xn, g_ref[...]).astype(o_ref.dtype)


def moe_combine(ys, e_flat, r_flat, off, route, x, g, *, tmc=256):
    t, d = x.shape
    tmc = min(tmc, t)
    n = t // tmc
    rowb = pl.BlockSpec((tmc, d), lambda i: (i, 0))
    cur = pl.BlockSpec((2 * tmc,), lambda i: (i,), memory_space=pltpu.SMEM)
    nxt = pl.BlockSpec((2 * tmc,), lambda i: (jnp.minimum(i + 1, n - 1),), memory_space=pltpu.SMEM)
    return pl.pallas_call(
        _combine_kernel,
        grid=(n,),
        in_specs=[cur, cur, nxt, nxt, pl.BlockSpec(memory_space=pltpu.SMEM),
                  pl.BlockSpec((tmc, LANES), lambda i: (i, 0)), rowb, pl.BlockSpec((1, d), lambda i: (0, 0)),
                  pl.BlockSpec(memory_space=pl.ANY)],
        out_specs=rowb,
        out_shape=jax.ShapeDtypeStruct((t, d), F32),
        scratch_shapes=[pltpu.VMEM((2, 2, tmc, d), F32), pltpu.SemaphoreType.DMA((2,))],
        compiler_params=_params(("arbitrary",), 48),
        name="moe_combine",
    )(e_flat, r_flat, e_flat, r_flat, off, route, x, g.reshape(1, d), ys)


def _tile_schedule(counts, tm, nt):
    cnt = counts.astype(jnp.int32)
    ntile = (cnt + tm - 1) // tm
    tend = jnp.cumsum(ntile)
    tstart = tend - ntile
    total = tend[-1]
    tid = jnp.arange(nt, dtype=jnp.int32)
    active = tid < total
    tclamp = jnp.minimum(tid, total - 1)
    te = jnp.sum((tclamp[:, None] >= tend[None, :]).astype(jnp.int32), axis=1)
    rows = jnp.clip(cnt[te] - (tclamp - tstart[te]) * tm, 0, tm)
    return tstart * tm, te, tclamp, jnp.where(active, rows, 0)


def moe_block(hn, route, counts, x, g, wg, wu, wd, *, tm=1024):
    t, d = hn.shape
    nt = -(-2 * t // tm) + N_EXPERTS
    off, te, tblk, trows = _tile_schedule(counts[0, :N_EXPERTS], tm, nt)
    e_flat = route[:, 0:2].astype(jnp.int32).reshape(-1)
    r_flat = route[:, 2:4].astype(jnp.int32).reshape(-1)
    xs = moe_dispatch(hn, e_flat, r_flat, off, trows, tm=tm)
    ys = moe_experts(xs, wg, wu, wd, te, tblk, trows, tm=tm)
    return moe_combine(ys, e_flat, r_flat, off, route, x, g)


def _block_diag(w):
    h, n, _ = w.shape
    eye = jnp.eye(h, dtype=w.dtype)
    return (eye[:, None, :, None] * w[:, :, None, :]).reshape(h * n, h * n)


def kernel(x, attn_norm, w_in, attn_sinks, conv_w, lru_conv_w, lru_conv_b, lru_wa, lru_ba, lru_wx,
           lru_bx, lru_lambda, mix_norm, w_out, ffn_norm, dense_w_gate, dense_w_up, dense_w_down,
           router_w, expert_w_gate, expert_w_up, expert_w_down, final_norm):
    b, s, d = x.shape
    depth = w_in.shape[0]
    assert depth == 2 and d == D_MODEL
    xf = x.reshape(b * s, d)
    out = None
    for layer in range(depth):
        proj = inproj(xf, attn_norm[layer], w_in[layer].astype(BF16))
        y = mixer(proj, attn_sinks[layer], conv_w[layer], lru_conv_w[layer], lru_conv_b[layer],
                  _block_diag(lru_wa[layer]).astype(BF16), lru_ba[layer],
                  _block_diag(lru_wx[layer]).astype(BF16), lru_bx[layer], lru_lambda[layer],
                  mix_norm[layer], batch=b)
        j = layer // 2
        if layer % 2 == 0:
            xf = outproj(y, w_out[layer].astype(BF16), xf)
            xf = ffn_dense(xf, ffn_norm[layer], dense_w_gate[j], dense_w_up[j], dense_w_down[j])
        else:
            rw = jnp.pad(router_w[j], ((0, 0), (0, LANES - N_EXPERTS)))
            rw_hi = rw.astype(BF16)
            rw_lo = (rw - rw_hi.astype(F32)).astype(BF16)
            rw2 = jnp.concatenate([jnp.concatenate([rw_hi, rw_lo], axis=1),
                                   jnp.concatenate([rw_hi, jnp.zeros_like(rw_lo)], axis=1)], axis=0)
            xf, hn, route, counts = outproj_route(y, w_out[layer].astype(BF16), xf, ffn_norm[layer], rw2)
            out = moe_block(hn, route, counts, xf, final_norm, expert_w_gate[j], expert_w_up[j],
                            expert_w_down[j])
    return out.reshape(b, s, d)
```

```python
import functools

import jax
import jax.numpy as jnp
from jax import lax
from jax.experimental import pallas as pl
from jax.experimental.pallas import tpu as pltpu

F32 = jnp.float32
BF16 = jnp.bfloat16

D_MODEL = 2048
HEAD_DIM = 64
D_ATTN = 1024
N_HEADS = 16
N_KV_HEADS = 2
GROUP = N_HEADS // N_KV_HEADS
BLOCK = 128
D_KV = N_KV_HEADS * HEAD_DIM
D_CONV = 512
D_LRU = 512
LRU_C = 8.0
LOG2E = 1.4426950408889634
D_IN = 3840
N_EXPERTS = 8
EPS = 1e-6
LANES = 128
SUBLANES = 8

O_Q, O_K, O_V = 0, 1024, 1152
O_CB, O_CC, O_CX = 1280, 1792, 2304
O_LX, O_LG = 2816, 3328

MIB = 2 ** 20


def _params(sem, vmem_mib):
    return pltpu.CompilerParams(dimension_semantics=sem, vmem_limit_bytes=vmem_mib * MIB)


def _rms(x, g):
    return x * lax.rsqrt(jnp.mean(x * x, axis=-1, keepdims=True) + EPS) * g


def _inproj_kernel(x_ref, g_ref, w_ref, o_ref, hb_ref):
    @pl.when(pl.program_id(1) == 0)
    def _():
        hb_ref[...] = _rms(x_ref[...], g_ref[...]).astype(BF16)

    o_ref[...] = jnp.dot(hb_ref[...], w_ref[...], preferred_element_type=F32)


def inproj(x, g, w, *, tm=1024, tn=1280):
    t, d = x.shape
    n = w.shape[1]
    tm = min(tm, t)
    return pl.pallas_call(
        _inproj_kernel,
        grid=(t // tm, n // tn),
        in_specs=[pl.BlockSpec((tm, d), lambda i, j: (i, 0)), pl.BlockSpec((1, d), lambda i, j: (0, 0)),
                  pl.BlockSpec((d, tn), lambda i, j: (0, j))],
        out_specs=pl.BlockSpec((tm, tn), lambda i, j: (i, j)),
        out_shape=jax.ShapeDtypeStruct((t, n), F32),
        scratch_shapes=[pltpu.VMEM((tm, d), BF16)],
        compiler_params=_params(("arbitrary", "arbitrary"), 56),
        name="inproj",
    )(x, g.reshape(1, d), w)


def _shift_rows(u, halo, k):
    r = pltpu.roll(u, k, axis=0)
    hr = pltpu.roll(halo, k, axis=0)
    row = lax.broadcasted_iota(jnp.int32, hr.shape, 0)
    top = jnp.where(row < k, hr, r[:SUBLANES])
    return jnp.concatenate([top, r[SUBLANES:]], axis=0)


def _linear_scan(a, b, h0):
    n, c = a.shape
    g = n // SUBLANES
    a = a.reshape(g, SUBLANES, c)
    b = b.reshape(g, SUBLANES, c)
    sub = lax.broadcasted_iota(jnp.int32, a.shape, 1)
    s = 1
    while s < SUBLANES:
        a_sh = pltpu.roll(a, s, axis=1)
        b_sh = pltpu.roll(b, s, axis=1)
        valid = sub >= s
        b = jnp.where(valid, a * b_sh, 0.0) + b
        a = jnp.where(valid, a * a_sh, a)
        s *= 2
    hs = []
    carry = h0
    for i in range(g):
        h = a[i] * carry + b[i]
        carry = h[SUBLANES - 1:SUBLANES, :]
        hs.append(h)
    return jnp.concatenate(hs, axis=0), carry


def _half_variants(x):
    lane = lax.broadcasted_iota(jnp.int32, x.shape, 1)
    lo = lane < HEAD_DIM
    sw = pltpu.roll(x, HEAD_DIM, axis=1)
    z = jnp.zeros_like(x)
    return [jnp.where(lo, x, z).astype(BF16), jnp.where(lo, z, sw).astype(BF16),
            jnp.where(lo, sw, z).astype(BF16), jnp.where(lo, z, x).astype(BF16)]


def _mixer_kernel(sinks_ref, proj_ref, convw_ref, lcw_ref, lcb_ref, wa_ref, ba_ref, wx_ref, bx_ref,
                  lam_ref, gn_ref, y_ref, kprev, vprev, uhalo, lxhalo, hcarry, yatt):
    tb = pl.program_id(1)
    tt = proj_ref.shape[0]

    @pl.when(tb == 0)
    def _():
        kprev[...] = jnp.zeros_like(kprev)
        vprev[...] = jnp.zeros_like(vprev)
        uhalo[...] = jnp.zeros_like(uhalo)
        lxhalo[...] = jnp.zeros_like(lxhalo)
        hcarry[...] = jnp.zeros_like(hcarry)

    rows4 = GROUP // 2 * BLOCK
    row = lax.broadcasted_iota(jnp.int32, (rows4, LANES), 0)
    col = lax.broadcasted_iota(jnp.int32, (rows4, LANES), 1)
    in_cur = col <= (row & (BLOCK - 1))
    even = col < HEAD_DIM
    even1 = lax.broadcasted_iota(jnp.int32, (BLOCK, LANES), 1) < HEAD_DIM
    ones_even = jnp.where(even1, 1.0, 0.0).astype(BF16)
    ones_odd = jnp.where(even1, 0.0, 1.0).astype(BF16)
    den_cols = jnp.concatenate([ones_even, ones_even, ones_odd, ones_odd], axis=0)
    first_bias = jnp.where(tb > 0, 0.0, -jnp.inf).astype(F32)
    for sb in range(tt // BLOCK):
        rows = slice(sb * BLOCK, (sb + 1) * BLOCK)
        kc = _half_variants(proj_ref[rows, O_K:O_K + D_KV])
        vc = _half_variants(proj_ref[rows, O_V:O_V + D_KV])
        if sb == 0:
            kp = [kprev[i] for i in range(4)]
            vp = [vprev[i] for i in range(4)]
        for kvh in range(N_KV_HEADS):
            pairs = range(kvh * GROUP // 2, (kvh + 1) * GROUP // 2)
            lo, hi = 2 * kvh, 2 * kvh + 1
            q4 = jnp.concatenate([proj_ref[rows, p * LANES:(p + 1) * LANES] for p in pairs], axis=0)
            q4 = (q4 * (HEAD_DIM ** -0.5 * LOG2E)).astype(BF16)
            kk = jnp.concatenate([kc[lo], kp[lo], kc[hi], kp[hi]], axis=0)
            s = lax.dot_general(q4, kk, (((1,), (1,)), ((), ())), preferred_element_type=F32)
            pps, sink_terms = [], []
            for par in range(2):
                c0 = 2 * BLOCK * par
                s_prev = s[:, c0 + BLOCK:c0 + 2 * BLOCK]
                if sb == 0:
                    s_prev = s_prev + first_bias
                ssel = jnp.where(in_cur, s[:, c0:c0 + BLOCK], s_prev)
                sk = [sinks_ref[2 * p + par] * LOG2E for p in pairs]
                sink = jnp.where(row < BLOCK, sk[0], jnp.where(row < 2 * BLOCK, sk[1],
                                 jnp.where(row < 3 * BLOCK, sk[2], sk[3])))
                m = jnp.maximum(jnp.max(ssel, axis=-1, keepdims=True), sink)
                p2 = jnp.exp2(ssel - m)
                pps += [jnp.where(in_cur, p2, 0.0).astype(BF16), jnp.where(in_cur, 0.0, p2).astype(BF16)]
                sink_terms.append(jnp.exp2(sink - m))
            vv = jnp.concatenate([vc[lo], vp[lo], vc[hi], vp[hi]], axis=0)
            r = jnp.dot(jnp.concatenate(pps, axis=1), jnp.concatenate([vv, den_cols], axis=1),
                        preferred_element_type=F32)
            den = r[:, LANES:] + jnp.where(even, sink_terms[0], sink_terms[1])
            o4 = r[:, :LANES] / den
            for idx, p in enumerate(pairs):
                yatt[rows, p * LANES:(p + 1) * LANES] = o4[idx * BLOCK:(idx + 1) * BLOCK]
        kp, vp = kc, vc
    for i in range(4):
        kprev[i] = kp[i]
        vprev[i] = vp[i]
    ya = yatt[...]
    y_ref[:, 0:D_ATTN] = _rms(ya, gn_ref[:, 0:D_ATTN]).astype(y_ref.dtype)

    u = proj_ref[:, O_CC:O_CC + D_CONV] * proj_ref[:, O_CX:O_CX + D_CONV]
    uh = uhalo[...]
    conv = (convw_ref[0:1, :] * u + convw_ref[1:2, :] * _shift_rows(u, uh, 1)
            + convw_ref[2:3, :] * _shift_rows(u, uh, 2))
    uhalo[...] = u[tt - SUBLANES:, :]
    yc = proj_ref[:, O_CB:O_CB + D_CONV] * conv
    y_ref[:, D_ATTN:D_ATTN + D_CONV] = _rms(yc, gn_ref[:, D_ATTN:D_ATTN + D_CONV]).astype(y_ref.dtype)

    lx = proj_ref[:, O_LX:O_LX + D_LRU]
    lh = lxhalo[...]
    xc = lcw_ref[0:1, :] * lx + lcb_ref[...]
    for k in range(1, 4):
        xc = xc + lcw_ref[k:k + 1, :] * _shift_rows(lx, lh, k)
    lxhalo[...] = lx[tt - SUBLANES:, :]
    xcb = xc.astype(BF16)
    r = jax.nn.sigmoid(jnp.dot(xcb, wa_ref[...], preferred_element_type=F32) + ba_ref[...])
    ig = jax.nn.sigmoid(jnp.dot(xcb, wx_ref[...], preferred_element_type=F32) + bx_ref[...])
    log_a = (-LRU_C) * r * jax.nn.softplus(-lam_ref[...])
    a = jnp.exp(log_a)
    th = jnp.tanh(log_a)
    bu = jnp.sqrt(-2.0 * th / (1.0 - th)) * (ig * xc)
    h, h_last = _linear_scan(a, bu, hcarry[0:1, :])
    hcarry[...] = jnp.broadcast_to(h_last, hcarry.shape)
    yl = h * jax.nn.gelu(proj_ref[:, O_LG:O_LG + D_LRU])
    y_ref[:, D_ATTN + D_CONV:] = _rms(yl, gn_ref[:, D_ATTN + D_CONV:]).astype(y_ref.dtype)


def mixer(proj, sinks, conv_w, lru_conv_w, lru_conv_b, wa_bd, ba, wx_bd, bx, lam, mix_norm,
          *, batch, tt=512):
    t = proj.shape[0]
    s = t // batch
    tt = min(tt, s)
    nt = s // tt
    row_map = lambda b, i: (b * nt + i, 0)
    full = lambda shape: pl.BlockSpec(shape, lambda b, i: (0,) * len(shape))
    return pl.pallas_call(
        _mixer_kernel,
        grid=(batch, nt),
        in_specs=[
            pl.BlockSpec(memory_space=pltpu.SMEM),
            pl.BlockSpec((tt, D_IN), row_map),
            full((3, D_CONV)), full((4, D_LRU)), full((1, D_LRU)),
            full((D_LRU, D_LRU)), full((1, D_LRU)), full((D_LRU, D_LRU)), full((1, D_LRU)),
            full((1, D_LRU)), full((1, D_MODEL)),
        ],
        out_specs=pl.BlockSpec((tt, D_MODEL), row_map),
        out_shape=jax.ShapeDtypeStruct((t, D_MODEL), BF16),
        scratch_shapes=[
            pltpu.VMEM((4, BLOCK, LANES), BF16), pltpu.VMEM((4, BLOCK, LANES), BF16),
            pltpu.VMEM((SUBLANES, D_CONV), F32), pltpu.VMEM((SUBLANES, D_LRU), F32),
            pltpu.VMEM((SUBLANES, D_LRU), F32), pltpu.VMEM((tt, D_ATTN), F32),
        ],
        compiler_params=_params(("arbitrary", "arbitrary"), 48),
        name="mixer",
    )(sinks, proj, conv_w, lru_conv_w, lru_conv_b.reshape(1, -1), wa_bd, ba.reshape(1, -1),
      wx_bd, bx.reshape(1, -1), lam.reshape(1, -1), mix_norm.reshape(1, -1))


def _route(hn, rw_ref, carry):
    tm = hn.shape[0]
    h_hi = hn.astype(BF16)
    h_lo = (hn - h_hi.astype(F32)).astype(BF16)
    res = jnp.dot(jnp.concatenate([h_hi, h_lo], axis=1), rw_ref[...], preferred_element_type=F32)
    logits = res[:, :LANES] + res[:, LANES:]
    lane = lax.broadcasted_iota(jnp.int32, (tm, LANES), 1).astype(F32)
    neg = -jnp.inf
    l1 = jnp.where(lane < N_EXPERTS, logits, neg)
    m1 = jnp.max(l1, axis=-1, keepdims=True)
    i1 = jnp.min(jnp.where(l1 == m1, lane, float(LANES)), axis=-1, keepdims=True)
    l2 = jnp.where(lane == i1, neg, l1)
    m2 = jnp.max(l2, axis=-1, keepdims=True)
    i2 = jnp.min(jnp.where(l2 == m2, lane, float(LANES)), axis=-1, keepdims=True)
    e21 = jnp.exp(m2 - m1)
    g1 = 1.0 / (1.0 + e21)
    g2 = e21 / (1.0 + e21)
    sel1 = lane == i1
    sel2 = lane == i2
    onehot = jnp.where(sel1 | sel2, 1.0, 0.0).astype(BF16)
    rr = lax.broadcasted_iota(jnp.int32, (tm, tm), 0)
    cc = lax.broadcasted_iota(jnp.int32, (tm, tm), 1)
    tri = jnp.where(rr >= cc, 1.0, 0.0).astype(BF16)
    counts = jnp.dot(tri, onehot, preferred_element_type=F32) + carry
    r1 = jnp.sum(jnp.where(sel1, counts, 0.0), axis=-1, keepdims=True) - 1.0
    r2 = jnp.sum(jnp.where(sel2, counts, 0.0), axis=-1, keepdims=True) - 1.0
    packed = jnp.where(lane == 0.0, i1, jnp.where(lane == 1.0, i2, jnp.where(lane == 2.0, r1,
             jnp.where(lane == 3.0, r2, jnp.where(lane == 4.0, g1, jnp.where(lane == 5.0, g2, 0.0))))))
    return packed, counts[tm - 1:tm, :]


def _outproj_kernel(y_ref, w_ref, x_ref, xo_ref):
    xo_ref[...] = x_ref[...] + jnp.dot(y_ref[...], w_ref[...], preferred_element_type=F32)


def _outproj_route_kernel(y_ref, w_ref, x_ref, g_ref, rw_ref, xo_ref, ho_ref, route_ref, cnt_ref,
                          carry, hprev):
    i = pl.program_id(0)

    @pl.when(i == 0)
    def _():
        carry[...] = jnp.zeros_like(carry)
        hprev[...] = jnp.zeros_like(hprev)

    packed, last = _route(hprev[...], rw_ref, carry[0:1, :])
    route_ref[...] = packed
    counts = jnp.where(i > 0, jnp.broadcast_to(last, carry.shape), carry[...])
    carry[...] = counts
    cnt_ref[...] = counts
    xn = x_ref[...] + jnp.dot(y_ref[...], w_ref[...], preferred_element_type=F32)
    xo_ref[...] = xn
    hn = _rms(xn, g_ref[...])
    ho_ref[...] = hn
    hprev[...] = hn


def outproj(y, w, x, *, tm=512):
    t, d = x.shape
    tm = min(tm, t)
    rowb = pl.BlockSpec((tm, d), lambda i: (i, 0))
    return pl.pallas_call(
        _outproj_kernel,
        grid=(t // tm,),
        in_specs=[rowb, pl.BlockSpec((d, d), lambda i: (0, 0)), rowb],
        out_specs=rowb,
        out_shape=jax.ShapeDtypeStruct((t, d), F32),
        compiler_params=_params(("arbitrary",), 48),
        name="outproj",
    )(y, w, x)


def outproj_route(y, w, x, g, rw2, *, tm=512):
    t, d = x.shape
    tm = min(tm, t)
    n = t // tm
    rowb = pl.BlockSpec((tm, d), lambda i: (jnp.minimum(i, n - 1), 0))
    return pl.pallas_call(
        _outproj_route_kernel,
        grid=(n + 1,),
        in_specs=[rowb, pl.BlockSpec((d, d), lambda i: (0, 0)), rowb, pl.BlockSpec((1, d), lambda i: (0, 0)),
                  pl.BlockSpec((2 * d, 2 * LANES), lambda i: (0, 0))],
        out_specs=[rowb, rowb, pl.BlockSpec((tm, LANES), lambda i: (jnp.maximum(i - 1, 0), 0)),
                   pl.BlockSpec((SUBLANES, LANES), lambda i: (0, 0))],
        out_shape=[jax.ShapeDtypeStruct((t, d), F32), jax.ShapeDtypeStruct((t, d), F32),
                   jax.ShapeDtypeStruct((t, LANES), F32), jax.ShapeDtypeStruct((SUBLANES, LANES), F32)],
        scratch_shapes=[pltpu.VMEM((SUBLANES, LANES), F32), pltpu.VMEM((tm, d), F32)],
        compiler_params=_params(("arbitrary",), 56),
        name="outproj_route",
    )(y, w, x, g.reshape(1, d), rw2)


def _swiglu_step(xb, wg, wu, wd):
    g = jnp.dot(xb, wg, preferred_element_type=F32)
    u = jnp.dot(xb, wu, preferred_element_type=F32)
    a = (g * jax.nn.sigmoid(g) * u).astype(BF16)
    return jnp.dot(a, wd, preferred_element_type=F32)


def _ffn_kernel(x_ref, g_ref, wg_ref, wu_ref, wd_ref, xo_ref, hb_ref):
    @pl.when(pl.program_id(1) == 0)
    def _():
        x = x_ref[...]
        hb_ref[...] = _rms(x, g_ref[...]).astype(BF16)
        xo_ref[...] = x

    xo_ref[...] += _swiglu_step(hb_ref[...], wg_ref[...].astype(BF16), wu_ref[...].astype(BF16),
                                wd_ref[...].astype(BF16))


def ffn_dense(x, g, wg, wu, wd, *, tm=1024, tf=256):
    t, d = x.shape
    ff = wg.shape[1]
    tm = min(tm, t)
    rowb = pl.BlockSpec((tm, d), lambda i, f: (i, 0))
    return pl.pallas_call(
        _ffn_kernel,
        grid=(t // tm, ff // tf),
        in_specs=[rowb, pl.BlockSpec((1, d), lambda i, f: (0, 0)),
                  pl.BlockSpec((d, tf), lambda i, f: (0, f)), pl.BlockSpec((d, tf), lambda i, f: (0, f)),
                  pl.BlockSpec((tf, d), lambda i, f: (f, 0))],
        out_specs=rowb,
        out_shape=jax.ShapeDtypeStruct((t, d), F32),
        scratch_shapes=[pltpu.VMEM((tm, d), BF16)],
        compiler_params=_params(("arbitrary", "arbitrary"), 58),
        name="ffn_dense",
    )(x, g.reshape(1, d), wg, wu, wd)


def _row_copy(src, src_row, dst, dst_row, sem):
    return pltpu.make_async_copy(src.at[pl.ds(src_row, 1)], dst.at[pl.ds(dst_row, 1)], sem)


def _for_rows(n, fn):
    def body(r, c):
        fn(r)
        return c

    lax.fori_loop(0, n, body, 0)


M_ALWAYS = 512
M_CHUNK = 256


def _expert_kernel(te_ref, tb_ref, nv_ref, xs_ref, wg_ref, wu_ref, wd_ref, ys_ref, xb, wgb, wub, wdb):
    i = pl.program_id(0)
    f = pl.program_id(1)
    tm = xb.shape[0]
    nv = nv_ref[i]

    @pl.when(nv > 0)
    def _():
        @pl.when(f == 0)
        def _():
            xb[...] = xs_ref[...].astype(BF16)
            ys_ref[...] = jnp.zeros_like(ys_ref)

        m0 = min(M_ALWAYS, tm)

        @pl.when(nv > tm - M_CHUNK)
        def _():
            ys_ref[...] += _swiglu_step(xb[...], wg_ref[...].astype(BF16), wu_ref[...].astype(BF16),
                                        wd_ref[...].astype(BF16))

        @pl.when(nv <= tm - M_CHUNK)
        def _():
            wgb[...] = wg_ref[...].astype(BF16)
            wub[...] = wu_ref[...].astype(BF16)
            wdb[...] = wd_ref[...].astype(BF16)
            ys_ref[0:m0, :] += _swiglu_step(xb[0:m0, :], wgb[...], wub[...], wdb[...])
            for c in range(m0, tm - M_CHUNK, M_CHUNK):
                @pl.when(c < nv)
                def _():
                    rows = pl.ds(c, M_CHUNK)
                    ys_ref[rows, :] += _swiglu_step(xb[rows, :], wgb[...], wub[...], wdb[...])

    @pl.when((nv == 0) & (f == pl.num_programs(1) - 1))
    def _():
        ys_ref[...] = jnp.zeros_like(ys_ref)


def moe_experts(xs, wg, wu, wd, tile_expert, tile_block, tile_rows, *, tm, tf=256):
    slots, d = xs.shape
    fe = wg.shape[2]
    nf = fe // tf
    nt = slots // tm

    def fidx(i, f, nv):
        return jnp.where(nv[i] > 0, f, nf - 1)

    grid_spec = pltpu.PrefetchScalarGridSpec(
        num_scalar_prefetch=3,
        grid=(nt, nf),
        in_specs=[
            pl.BlockSpec((tm, d), lambda i, f, te, tb, nv: (tb[i], 0)),
            pl.BlockSpec((None, d, tf), lambda i, f, te, tb, nv: (te[i], 0, fidx(i, f, nv))),
            pl.BlockSpec((None, d, tf), lambda i, f, te, tb, nv: (te[i], 0, fidx(i, f, nv))),
            pl.BlockSpec((None, tf, d), lambda i, f, te, tb, nv: (te[i], fidx(i, f, nv), 0)),
        ],
        out_specs=pl.BlockSpec((tm, d), lambda i, f, te, tb, nv: (i, 0)),
        scratch_shapes=[pltpu.VMEM((tm, d), BF16), pltpu.VMEM((d, tf), BF16), pltpu.VMEM((d, tf), BF16),
                        pltpu.VMEM((tf, d), BF16)],
    )
    return pl.pallas_call(
        _expert_kernel,
        grid_spec=grid_spec,
        out_shape=jax.ShapeDtypeStruct((slots, d), F32),
        compiler_params=_params(("arbitrary", "arbitrary"), 58),
        name="moe_experts",
    )(tile_expert, tile_block, tile_rows, xs, wg, wu, wd)


def _dispatch_kernel(pos_ref, trows_ref, h_ref, xs_hbm, zbuf, sem, zsem):
    tmd = h_ref.shape[0]
    tm = zbuf.shape[0]

    @pl.when(pl.program_id(0) == 0)
    def _():
        zbuf[...] = jnp.zeros_like(zbuf)

        def tile_fill(j):
            return pltpu.make_async_copy(zbuf, xs_hbm.at[pl.ds(j * tm, tm)], zsem)

        def fill(j):
            @pl.when(trows_ref[j] < tm)
            def _():
                tile_fill(j).start()

        def fill_wait(j):
            @pl.when(trows_ref[j] < tm)
            def _():
                tile_fill(j).wait()

        _for_rows(trows_ref.shape[0], fill)
        _for_rows(trows_ref.shape[0], fill_wait)

    for tok in range(tmd):
        for k in range(2):
            _row_copy(h_ref, tok, xs_hbm, pos_ref[2 * tok + k], sem).start()
    for _ in range(2):
        pltpu.make_async_copy(h_ref, xs_hbm.at[pl.ds(0, tmd)], sem).wait()


def moe_dispatch(hn, pos, tile_rows, *, tm, tmd=512):
    t, d = hn.shape
    tmd = min(tmd, t)
    return pl.pallas_call(
        _dispatch_kernel,
        grid=(t // tmd,),
        in_specs=[pl.BlockSpec((2 * tmd,), lambda i: (i,), memory_space=pltpu.SMEM),
                  pl.BlockSpec(memory_space=pltpu.SMEM),
                  pl.BlockSpec((tmd, d), lambda i: (i, 0))],
        out_specs=pl.BlockSpec(memory_space=pl.ANY),
        out_shape=jax.ShapeDtypeStruct((tile_rows.shape[0] * tm, d), hn.dtype),
        scratch_shapes=[pltpu.VMEM((tm, d), hn.dtype), pltpu.SemaphoreType.DMA(()),
                        pltpu.SemaphoreType.DMA(())],
        compiler_params=_params(("arbitrary",), 32),
        name="moe_dispatch",
    )(pos, tile_rows, hn)


def _combine_kernel(pos_ref, posn_ref, route_ref, x_ref, g_ref, ys_hbm, o_ref, buf, sem):
    i = pl.program_id(0)
    n = pl.num_programs(0)
    tmc = x_ref.shape[0]

    def gather(pref, slot):
        for tok in range(tmc):
            for k in range(2):
                _row_copy(ys_hbm, pref[2 * tok + k], buf.at[slot, k], tok, sem.at[slot]).start()

    @pl.when(i == 0)
    def _():
        gather(pos_ref, 0)

    @pl.when(i + 1 < n)
    def _():
        gather(posn_ref, (i + 1) % 2)

    cur = i % 2
    for k in range(2):
        pltpu.make_async_copy(ys_hbm.at[pl.ds(0, tmc)], buf.at[cur, k], sem.at[cur]).wait()
    xn = x_ref[...] + (route_ref[:, 4:5] * buf[cur, 0] + route_ref[:, 5:6] * buf[cur, 1])
    o_ref[...] = _rms(xn, g_ref[...]).astype(o_ref.dtype)


def moe_combine(ys, pos, route, x, g, *, tmc=256):
    t, d = x.shape
    tmc = min(tmc, t)
    n = t // tmc
    rowb = pl.BlockSpec((tmc, d), lambda i: (i, 0))
    cur = pl.BlockSpec((2 * tmc,), lambda i: (i,), memory_space=pltpu.SMEM)
    nxt = pl.BlockSpec((2 * tmc,), lambda i: (jnp.minimum(i + 1, n - 1),), memory_space=pltpu.SMEM)
    return pl.pallas_call(
        _combine_kernel,
        grid=(n,),
        in_specs=[cur, nxt,
                  pl.BlockSpec((tmc, LANES), lambda i: (i, 0)), rowb, pl.BlockSpec((1, d), lambda i: (0, 0)),
                  pl.BlockSpec(memory_space=pl.ANY)],
        out_specs=rowb,
        out_shape=jax.ShapeDtypeStruct((t, d), F32),
        scratch_shapes=[pltpu.VMEM((2, 2, tmc, d), F32), pltpu.SemaphoreType.DMA((2,))],
        compiler_params=_params(("arbitrary",), 48),
        name="moe_combine",
    )(pos, pos, route, x, g.reshape(1, d), ys)


def _tile_schedule(counts, tm, nt):
    cnt = counts.astype(jnp.int32)
    ntile = (cnt + tm - 1) // tm
    tend = jnp.cumsum(ntile)
    tstart = tend - ntile
    total = tend[-1]
    tid = jnp.arange(nt, dtype=jnp.int32)
    active = tid < total
    tclamp = jnp.minimum(tid, total - 1)
    te = jnp.sum((tclamp[:, None] >= tend[None, :]).astype(jnp.int32), axis=1)
    rows = jnp.clip(cnt[te] - (tclamp - tstart[te]) * tm, 0, tm)
    return tstart * tm, te, tclamp, jnp.where(active, rows, 0)


def moe_block(hn, route, counts, x, g, wg, wu, wd, *, tm=1024):
    t, d = hn.shape
    nt = -(-2 * t // tm) + N_EXPERTS
    off, te, tblk, trows = _tile_schedule(counts[0, :N_EXPERTS], tm, nt)
    pos = (off[route[:, 0:2].astype(jnp.int32)] + route[:, 2:4].astype(jnp.int32)).reshape(-1)
    xs = moe_dispatch(hn, pos, trows, tm=tm)
    ys = moe_experts(xs, wg, wu, wd, te, tblk, trows, tm=tm)
    return moe_combine(ys, pos, route, x, g)


def _block_diag(w):
    h, n, _ = w.shape
    eye = jnp.eye(h, dtype=w.dtype)
    return (eye[:, None, :, None] * w[:, :, None, :]).reshape(h * n, h * n)


def kernel(x, attn_norm, w_in, attn_sinks, conv_w, lru_conv_w, lru_conv_b, lru_wa, lru_ba, lru_wx,
           lru_bx, lru_lambda, mix_norm, w_out, ffn_norm, dense_w_gate, dense_w_up, dense_w_down,
           router_w, expert_w_gate, expert_w_up, expert_w_down, final_norm):
    b, s, d = x.shape
    depth = w_in.shape[0]
    assert depth == 2 and d == D_MODEL
    xf = x.reshape(b * s, d)
    out = None
    for layer in range(depth):
        proj = inproj(xf, attn_norm[layer], w_in[layer].astype(BF16))
        y = mixer(proj, attn_sinks[layer], conv_w[layer], lru_conv_w[layer], lru_conv_b[layer],
                  _block_diag(lru_wa[layer]).astype(BF16), lru_ba[layer],
                  _block_diag(lru_wx[layer]).astype(BF16), lru_bx[layer], lru_lambda[layer],
                  mix_norm[layer], batch=b)
        j = layer // 2
        if layer % 2 == 0:
            xf = outproj(y, w_out[layer].astype(BF16), xf)
            xf = ffn_dense(xf, ffn_norm[layer], dense_w_gate[j], dense_w_up[j], dense_w_down[j])
        else:
            rw = jnp.pad(router_w[j], ((0, 0), (0, LANES - N_EXPERTS)))
            rw_hi = rw.astype(BF16)
            rw_lo = (rw - rw_hi.astype(F32)).astype(BF16)
            rw2 = jnp.concatenate([jnp.concatenate([rw_hi, rw_lo], axis=1),
                                   jnp.concatenate([rw_hi, jnp.zeros_like(rw_lo)], axis=1)], axis=0)
            xf, hn, route, counts = outproj_route(y, w_out[layer].astype(BF16), xf, ffn_norm[layer], rw2)
            out = moe_block(hn, route, counts, xf, final_norm, expert_w_gate[j], expert_w_up[j],
                            expert_w_down[j])
    return out.reshape(b, s, d)
```

```python
import functools

import jax
import jax.numpy as jnp
from jax import lax
from jax.experimental import pallas as pl
from jax.experimental.pallas import tpu as pltpu

F32 = jnp.float32
BF16 = jnp.bfloat16

D_MODEL = 2048
HEAD_DIM = 64
D_ATTN = 1024
N_HEADS = 16
N_KV_HEADS = 2
GROUP = N_HEADS // N_KV_HEADS
BLOCK = 128
D_KV = N_KV_HEADS * HEAD_DIM
D_CONV = 512
D_LRU = 512
LRU_C = 8.0
LOG2E = 1.4426950408889634
D_IN = 3840
N_EXPERTS = 8
EPS = 1e-6
LANES = 128
SUBLANES = 8

O_Q, O_K, O_V = 0, 1024, 1152
O_CB, O_CC, O_CX = 1280, 1792, 2304
O_LX, O_LG = 2816, 3328

MIB = 2 ** 20


def _params(sem, vmem_mib):
    return pltpu.CompilerParams(dimension_semantics=sem, vmem_limit_bytes=vmem_mib * MIB)


def _rms(x, g):
    return x * lax.rsqrt(jnp.mean(x * x, axis=-1, keepdims=True) + EPS) * g


def _inproj_kernel(x_ref, g_ref, w_ref, o_ref, hb_ref):
    @pl.when(pl.program_id(1) == 0)
    def _():
        hb_ref[...] = _rms(x_ref[...], g_ref[...]).astype(BF16)

    o_ref[...] = jnp.dot(hb_ref[...], w_ref[...], preferred_element_type=F32)


def inproj(x, g, w, *, tm=1024, tn=1280):
    t, d = x.shape
    n = w.shape[1]
    tm = min(tm, t)
    return pl.pallas_call(
        _inproj_kernel,
        grid=(t // tm, n // tn),
        in_specs=[pl.BlockSpec((tm, d), lambda i, j: (i, 0)), pl.BlockSpec((1, d), lambda i, j: (0, 0)),
                  pl.BlockSpec((d, tn), lambda i, j: (0, j))],
        out_specs=pl.BlockSpec((tm, tn), lambda i, j: (i, j)),
        out_shape=jax.ShapeDtypeStruct((t, n), F32),
        scratch_shapes=[pltpu.VMEM((tm, d), BF16)],
        compiler_params=_params(("arbitrary", "arbitrary"), 56),
        name="inproj",
    )(x, g.reshape(1, d), w)


def _shift_rows(u, halo, k):
    r = pltpu.roll(u, k, axis=0)
    hr = pltpu.roll(halo, k, axis=0)
    row = lax.broadcasted_iota(jnp.int32, hr.shape, 0)
    top = jnp.where(row < k, hr, r[:SUBLANES])
    return jnp.concatenate([top, r[SUBLANES:]], axis=0)


def _linear_scan(a, b, h0):
    n, c = a.shape
    g = n // SUBLANES
    a = a.reshape(g, SUBLANES, c)
    b = b.reshape(g, SUBLANES, c)
    sub = lax.broadcasted_iota(jnp.int32, a.shape, 1)
    s = 1
    while s < SUBLANES:
        a_sh = pltpu.roll(a, s, axis=1)
        b_sh = pltpu.roll(b, s, axis=1)
        valid = sub >= s
        b = jnp.where(valid, a * b_sh, 0.0) + b
        a = jnp.where(valid, a * a_sh, a)
        s *= 2
    hs = []
    carry = h0
    for i in range(g):
        h = a[i] * carry + b[i]
        carry = h[SUBLANES - 1:SUBLANES, :]
        hs.append(h)
    return jnp.concatenate(hs, axis=0), carry


def _half_variants(x):
    lane = lax.broadcasted_iota(jnp.int32, x.shape, 1)
    lo = lane < HEAD_DIM
    sw = pltpu.roll(x, HEAD_DIM, axis=1)
    z = jnp.zeros_like(x)
    return [jnp.where(lo, x, z).astype(BF16), jnp.where(lo, z, sw).astype(BF16),
            jnp.where(lo, sw, z).astype(BF16), jnp.where(lo, z, x).astype(BF16)]


def _mixer_kernel(sinks_ref, proj_ref, convw_ref, lcw_ref, lcb_ref, wa_ref, ba_ref, wx_ref, bx_ref,
                  lam_ref, gn_ref, y_ref, kprev, vprev, uhalo, lxhalo, hcarry, yatt):
    tb = pl.program_id(1)
    tt = proj_ref.shape[0]

    @pl.when(tb == 0)
    def _():
        kprev[...] = jnp.zeros_like(kprev)
        vprev[...] = jnp.zeros_like(vprev)
        uhalo[...] = jnp.zeros_like(uhalo)
        lxhalo[...] = jnp.zeros_like(lxhalo)
        hcarry[...] = jnp.zeros_like(hcarry)

    rows4 = GROUP // 2 * BLOCK
    row = lax.broadcasted_iota(jnp.int32, (rows4, LANES), 0)
    col = lax.broadcasted_iota(jnp.int32, (rows4, LANES), 1)
    in_cur = col <= (row & (BLOCK - 1))
    even = col < HEAD_DIM
    even1 = lax.broadcasted_iota(jnp.int32, (BLOCK, LANES), 1) < HEAD_DIM
    ones_even = jnp.where(even1, 1.0, 0.0).astype(BF16)
    ones_odd = jnp.where(even1, 0.0, 1.0).astype(BF16)
    den_cols = jnp.concatenate([ones_even, ones_even, ones_odd, ones_odd], axis=0)
    first_bias = jnp.where(tb > 0, 0.0, -jnp.inf).astype(F32)
    for sb in range(tt // BLOCK):
        rows = slice(sb * BLOCK, (sb + 1) * BLOCK)
        kc = _half_variants(proj_ref[rows, O_K:O_K + D_KV])
        vc = _half_variants(proj_ref[rows, O_V:O_V + D_KV])
        if sb == 0:
            kp = [kprev[i] for i in range(4)]
            vp = [vprev[i] for i in range(4)]
        for kvh in range(N_KV_HEADS):
            pairs = range(kvh * GROUP // 2, (kvh + 1) * GROUP // 2)
            lo, hi = 2 * kvh, 2 * kvh + 1
            q4 = jnp.concatenate([proj_ref[rows, p * LANES:(p + 1) * LANES] for p in pairs], axis=0)
            q4 = (q4 * (HEAD_DIM ** -0.5 * LOG2E)).astype(BF16)
            kk = jnp.concatenate([kc[lo], kp[lo], kc[hi], kp[hi]], axis=0)
            s = lax.dot_general(q4, kk, (((1,), (1,)), ((), ())), preferred_element_type=F32)
            pps, sink_terms = [], []
            for par in range(2):
                c0 = 2 * BLOCK * par
                s_prev = s[:, c0 + BLOCK:c0 + 2 * BLOCK]
                if sb == 0:
                    s_prev = s_prev + first_bias
                ssel = jnp.where(in_cur, s[:, c0:c0 + BLOCK], s_prev)
                sk = [sinks_ref[2 * p + par] * LOG2E for p in pairs]
                sink = jnp.where(row < BLOCK, sk[0], jnp.where(row < 2 * BLOCK, sk[1],
                                 jnp.where(row < 3 * BLOCK, sk[2], sk[3])))
                m = jnp.maximum(jnp.max(ssel, axis=-1, keepdims=True), sink)
                p2 = jnp.exp2(ssel - m)
                pps += [jnp.where(in_cur, p2, 0.0).astype(BF16), jnp.where(in_cur, 0.0, p2).astype(BF16)]
                sink_terms.append(jnp.exp2(sink - m))
            vv = jnp.concatenate([vc[lo], vp[lo], vc[hi], vp[hi]], axis=0)
            r = jnp.dot(jnp.concatenate(pps, axis=1), jnp.concatenate([vv, den_cols], axis=1),
                        preferred_element_type=F32)
            den = r[:, LANES:] + jnp.where(even, sink_terms[0], sink_terms[1])
            o4 = r[:, :LANES] / den
            for idx, p in enumerate(pairs):
                yatt[rows, p * LANES:(p + 1) * LANES] = o4[idx * BLOCK:(idx + 1) * BLOCK]
        kp, vp = kc, vc
    for i in range(4):
        kprev[i] = kp[i]
        vprev[i] = vp[i]
    ya = yatt[...]
    y_ref[:, 0:D_ATTN] = _rms(ya, gn_ref[:, 0:D_ATTN]).astype(y_ref.dtype)

    u = proj_ref[:, O_CC:O_CC + D_CONV] * proj_ref[:, O_CX:O_CX + D_CONV]
    uh = uhalo[...]
    conv = (convw_ref[0:1, :] * u + convw_ref[1:2, :] * _shift_rows(u, uh, 1)
            + convw_ref[2:3, :] * _shift_rows(u, uh, 2))
    uhalo[...] = u[tt - SUBLANES:, :]
    yc = proj_ref[:, O_CB:O_CB + D_CONV] * conv
    y_ref[:, D_ATTN:D_ATTN + D_CONV] = _rms(yc, gn_ref[:, D_ATTN:D_ATTN + D_CONV]).astype(y_ref.dtype)

    lx = proj_ref[:, O_LX:O_LX + D_LRU]
    lh = lxhalo[...]
    xc = lcw_ref[0:1, :] * lx + lcb_ref[...]
    for k in range(1, 4):
        xc = xc + lcw_ref[k:k + 1, :] * _shift_rows(lx, lh, k)
    lxhalo[...] = lx[tt - SUBLANES:, :]
    xcb = xc.astype(BF16)
    r = jax.nn.sigmoid(jnp.dot(xcb, wa_ref[...], preferred_element_type=F32) + ba_ref[...])
    ig = jax.nn.sigmoid(jnp.dot(xcb, wx_ref[...], preferred_element_type=F32) + bx_ref[...])
    log_a = (-LRU_C) * r * jax.nn.softplus(-lam_ref[...])
    a = jnp.exp(log_a)
    th = jnp.tanh(log_a)
    bu = jnp.sqrt(-2.0 * th / (1.0 - th)) * (ig * xc)
    h, h_last = _linear_scan(a, bu, hcarry[0:1, :])
    hcarry[...] = jnp.broadcast_to(h_last, hcarry.shape)
    yl = h * jax.nn.gelu(proj_ref[:, O_LG:O_LG + D_LRU])
    y_ref[:, D_ATTN + D_CONV:] = _rms(yl, gn_ref[:, D_ATTN + D_CONV:]).astype(y_ref.dtype)


def mixer(proj, sinks, conv_w, lru_conv_w, lru_conv_b, wa_bd, ba, wx_bd, bx, lam, mix_norm,
          *, batch, tt=512):
    t = proj.shape[0]
    s = t // batch
    tt = min(tt, s)
    nt = s // tt
    row_map = lambda b, i: (b * nt + i, 0)
    full = lambda shape: pl.BlockSpec(shape, lambda b, i: (0,) * len(shape))
    return pl.pallas_call(
        _mixer_kernel,
        grid=(batch, nt),
        in_specs=[
            pl.BlockSpec(memory_space=pltpu.SMEM),
            pl.BlockSpec((tt, D_IN), row_map),
            full((3, D_CONV)), full((4, D_LRU)), full((1, D_LRU)),
            full((D_LRU, D_LRU)), full((1, D_LRU)), full((D_LRU, D_LRU)), full((1, D_LRU)),
            full((1, D_LRU)), full((1, D_MODEL)),
        ],
        out_specs=pl.BlockSpec((tt, D_MODEL), row_map),
        out_shape=jax.ShapeDtypeStruct((t, D_MODEL), BF16),
        scratch_shapes=[
            pltpu.VMEM((4, BLOCK, LANES), BF16), pltpu.VMEM((4, BLOCK, LANES), BF16),
            pltpu.VMEM((SUBLANES, D_CONV), F32), pltpu.VMEM((SUBLANES, D_LRU), F32),
            pltpu.VMEM((SUBLANES, D_LRU), F32), pltpu.VMEM((tt, D_ATTN), F32),
        ],
        compiler_params=_params(("arbitrary", "arbitrary"), 48),
        name="mixer",
    )(sinks, proj, conv_w, lru_conv_w, lru_conv_b.reshape(1, -1), wa_bd, ba.reshape(1, -1),
      wx_bd, bx.reshape(1, -1), lam.reshape(1, -1), mix_norm.reshape(1, -1))


def _route(hn, rw_ref, carry):
    tm = hn.shape[0]
    h_hi = hn.astype(BF16)
    h_lo = (hn - h_hi.astype(F32)).astype(BF16)
    res = jnp.dot(jnp.concatenate([h_hi, h_lo], axis=1), rw_ref[...], preferred_element_type=F32)
    logits = res[:, :LANES] + res[:, LANES:]
    lane = lax.broadcasted_iota(jnp.int32, (tm, LANES), 1).astype(F32)
    neg = -jnp.inf
    l1 = jnp.where(lane < N_EXPERTS, logits, neg)
    m1 = jnp.max(l1, axis=-1, keepdims=True)
    i1 = jnp.min(jnp.where(l1 == m1, lane, float(LANES)), axis=-1, keepdims=True)
    l2 = jnp.where(lane == i1, neg, l1)
    m2 = jnp.max(l2, axis=-1, keepdims=True)
    i2 = jnp.min(jnp.where(l2 == m2, lane, float(LANES)), axis=-1, keepdims=True)
    e21 = jnp.exp(m2 - m1)
    g1 = 1.0 / (1.0 + e21)
    g2 = e21 / (1.0 + e21)
    sel1 = lane == i1
    sel2 = lane == i2
    onehot = jnp.where(sel1 | sel2, 1.0, 0.0).astype(BF16)
    rr = lax.broadcasted_iota(jnp.int32, (tm, tm), 0)
    cc = lax.broadcasted_iota(jnp.int32, (tm, tm), 1)
    tri = jnp.where(rr >= cc, 1.0, 0.0).astype(BF16)
    counts = jnp.dot(tri, onehot, preferred_element_type=F32) + carry
    r1 = jnp.sum(jnp.where(sel1, counts, 0.0), axis=-1, keepdims=True) - 1.0
    r2 = jnp.sum(jnp.where(sel2, counts, 0.0), axis=-1, keepdims=True) - 1.0
    packed = jnp.where(lane == 0.0, i1, jnp.where(lane == 1.0, i2, jnp.where(lane == 2.0, r1,
             jnp.where(lane == 3.0, r2, jnp.where(lane == 4.0, g1, jnp.where(lane == 5.0, g2, 0.0))))))
    return packed, counts[tm - 1:tm, :]


def _outproj_kernel(y_ref, w_ref, x_ref, xo_ref):
    xo_ref[...] = x_ref[...] + jnp.dot(y_ref[...], w_ref[...], preferred_element_type=F32)


def _outproj_route_kernel(y_ref, w_ref, x_ref, g_ref, rw_ref, xo_ref, ho_ref, route_ref, cnt_ref,
                          carry, hprev):
    i = pl.program_id(0)

    @pl.when(i == 0)
    def _():
        carry[...] = jnp.zeros_like(carry)
        hprev[...] = jnp.zeros_like(hprev)

    packed, last = _route(hprev[...], rw_ref, carry[0:1, :])
    route_ref[...] = packed
    counts = jnp.where(i > 0, jnp.broadcast_to(last, carry.shape), carry[...])
    carry[...] = counts
    cnt_ref[...] = counts
    xn = x_ref[...] + jnp.dot(y_ref[...], w_ref[...], preferred_element_type=F32)
    xo_ref[...] = xn
    hn = _rms(xn, g_ref[...])
    ho_ref[...] = hn
    hprev[...] = hn


def outproj(y, w, x, *, tm=512):
    t, d = x.shape
    tm = min(tm, t)
    rowb = pl.BlockSpec((tm, d), lambda i: (i, 0))
    return pl.pallas_call(
        _outproj_kernel,
        grid=(t // tm,),
        in_specs=[rowb, pl.BlockSpec((d, d), lambda i: (0, 0)), rowb],
        out_specs=rowb,
        out_shape=jax.ShapeDtypeStruct((t, d), F32),
        compiler_params=_params(("arbitrary",), 48),
        name="outproj",
    )(y, w, x)


def outproj_route(y, w, x, g, rw2, *, tm=512):
    t, d = x.shape
    tm = min(tm, t)
    n = t // tm
    rowb = pl.BlockSpec((tm, d), lambda i: (jnp.minimum(i, n - 1), 0))
    return pl.pallas_call(
        _outproj_route_kernel,
        grid=(n + 1,),
        in_specs=[rowb, pl.BlockSpec((d, d), lambda i: (0, 0)), rowb, pl.BlockSpec((1, d), lambda i: (0, 0)),
                  pl.BlockSpec((2 * d, 2 * LANES), lambda i: (0, 0))],
        out_specs=[rowb, rowb, pl.BlockSpec((tm, LANES), lambda i: (jnp.maximum(i - 1, 0), 0)),
                   pl.BlockSpec((SUBLANES, LANES), lambda i: (0, 0))],
        out_shape=[jax.ShapeDtypeStruct((t, d), F32), jax.ShapeDtypeStruct((t, d), F32),
                   jax.ShapeDtypeStruct((t, LANES), F32), jax.ShapeDtypeStruct((SUBLANES, LANES), F32)],
        scratch_shapes=[pltpu.VMEM((SUBLANES, LANES), F32), pltpu.VMEM((tm, d), F32)],
        compiler_params=_params(("arbitrary",), 56),
        name="outproj_route",
    )(y, w, x, g.reshape(1, d), rw2)


def _swiglu_step(xb, wg, wu, wd):
    g = jnp.dot(xb, wg, preferred_element_type=F32)
    u = jnp.dot(xb, wu, preferred_element_type=F32)
    a = (g * jax.nn.sigmoid(g) * u).astype(BF16)
    return jnp.dot(a, wd, preferred_element_type=F32)


W_SLOTS = 3


def _weight_copies(wg_hbm, wu_hbm, wd_hbm, f, tf, wgr, wur, wdr, sem, slot, expert=None):
    cols = pl.ds(pl.multiple_of(f * tf, tf), tf)
    if expert is not None:
        wg_hbm, wu_hbm, wd_hbm = wg_hbm.at[expert], wu_hbm.at[expert], wd_hbm.at[expert]
    return (pltpu.make_async_copy(wg_hbm.at[:, cols], wgr.at[slot], sem.at[0, slot]),
            pltpu.make_async_copy(wu_hbm.at[:, cols], wur.at[slot], sem.at[1, slot]),
            pltpu.make_async_copy(wd_hbm.at[cols, :], wdr.at[slot], sem.at[2, slot]))


def _ffn_kernel(x_ref, g_ref, wg_hbm, wu_hbm, wd_hbm, xo_ref, hb_ref, wgr, wur, wdr, sem):
    i = pl.program_id(0)
    f = pl.program_id(1)
    nf = pl.num_programs(1)
    tf = wgr.shape[2]
    step = i * nf + f
    last = pl.num_programs(0) * nf - 1

    def fetch(s):
        for cp in _weight_copies(wg_hbm, wu_hbm, wd_hbm, lax.rem(s, nf), tf, wgr, wur, wdr, sem,
                                 lax.rem(s, W_SLOTS)):
            cp.start()

    @pl.when(step == 0)
    def _():
        fetch(step)
        fetch(step + 1)

    @pl.when(step + 2 <= last)
    def _():
        fetch(step + 2)

    @pl.when(f == 0)
    def _():
        x = x_ref[...]
        hb_ref[...] = _rms(x, g_ref[...]).astype(BF16)
        xo_ref[...] = x

    slot = lax.rem(step, W_SLOTS)
    for cp in _weight_copies(wg_hbm, wu_hbm, wd_hbm, f, tf, wgr, wur, wdr, sem, slot):
        cp.wait()
    xo_ref[...] += _swiglu_step(hb_ref[...], wgr[slot].astype(BF16), wur[slot].astype(BF16),
                                wdr[slot].astype(BF16))


def ffn_dense(x, g, wg, wu, wd, *, tm=1024, tf=256):
    t, d = x.shape
    ff = wg.shape[1]
    tm = min(tm, t)
    rowb = pl.BlockSpec((tm, d), lambda i, f: (i, 0))
    return pl.pallas_call(
        _ffn_kernel,
        grid=(t // tm, ff // tf),
        in_specs=[rowb, pl.BlockSpec((1, d), lambda i, f: (0, 0)),
                  pl.BlockSpec(memory_space=pl.ANY), pl.BlockSpec(memory_space=pl.ANY),
                  pl.BlockSpec(memory_space=pl.ANY)],
        out_specs=rowb,
        out_shape=jax.ShapeDtypeStruct((t, d), F32),
        scratch_shapes=[pltpu.VMEM((tm, d), BF16), pltpu.VMEM((W_SLOTS, d, tf), F32),
                        pltpu.VMEM((W_SLOTS, d, tf), F32), pltpu.VMEM((W_SLOTS, tf, d), F32),
                        pltpu.SemaphoreType.DMA((3, W_SLOTS))],
        compiler_params=_params(("arbitrary", "arbitrary"), 60),
        name="ffn_dense",
    )(x, g.reshape(1, d), wg, wu, wd)


def _row_copy(src, src_row, dst, dst_row, sem):
    return pltpu.make_async_copy(src.at[pl.ds(src_row, 1)], dst.at[pl.ds(dst_row, 1)], sem)


def _for_rows(n, fn):
    def body(r, c):
        fn(r)
        return c

    lax.fori_loop(0, n, body, 0)


M_ALWAYS = 512
M_CHUNK = 256


def _expert_kernel(te_ref, tb_ref, nv_ref, xs_ref, wg_hbm, wu_hbm, wd_hbm, ys_ref, xb, wgr, wur, wdr, sem):
    i = pl.program_id(0)
    f = pl.program_id(1)
    nt = pl.num_programs(0)
    nf = pl.num_programs(1)
    tm = xb.shape[0]
    tf = wgr.shape[2]
    nv = nv_ref[i]
    step = i * nf + f

    def copies(tile, fs, s):
        return _weight_copies(wg_hbm, wu_hbm, wd_hbm, fs, tf, wgr, wur, wdr, sem, lax.rem(s, W_SLOTS),
                              expert=te_ref[tile])

    def fetch_ahead(k):
        over = f + k >= nf
        tile = jnp.minimum(i + over.astype(jnp.int32), nt - 1)
        fs = f + k - nf * over.astype(jnp.int32)

        @pl.when(jnp.logical_not(over) | ((i + 1 < nt) & (nv_ref[tile] > 0)))
        def _():
            for cp in copies(tile, fs, step + k):
                cp.start()

    @pl.when(nv > 0)
    def _():
        @pl.when(step == 0)
        def _():
            fetch_ahead(0)
            fetch_ahead(1)

        fetch_ahead(2)

        @pl.when(f == 0)
        def _():
            xb[...] = xs_ref[...].astype(BF16)
            ys_ref[...] = jnp.zeros_like(ys_ref)

        slot = lax.rem(step, W_SLOTS)
        for cp in copies(i, f, step):
            cp.wait()
        m0 = min(M_ALWAYS, tm)

        @pl.when(nv > tm - M_CHUNK)
        def _():
            ys_ref[...] += _swiglu_step(xb[...], wgr[slot].astype(BF16), wur[slot].astype(BF16),
                                        wdr[slot].astype(BF16))

        @pl.when(nv <= tm - M_CHUNK)
        def _():
            ys_ref[0:m0, :] += _swiglu_step(xb[0:m0, :], wgr[slot].astype(BF16), wur[slot].astype(BF16),
                                            wdr[slot].astype(BF16))
            for c in range(m0, tm - M_CHUNK, M_CHUNK):
                @pl.when(c < nv)
                def _():
                    rows = pl.ds(c, M_CHUNK)
                    ys_ref[rows, :] += _swiglu_step(xb[rows, :], wgr[slot].astype(BF16),
                                                    wur[slot].astype(BF16), wdr[slot].astype(BF16))

    @pl.when((nv == 0) & (f == nf - 1))
    def _():
        ys_ref[...] = jnp.zeros_like(ys_ref)


def moe_experts(xs, wg, wu, wd, tile_expert, tile_block, tile_rows, *, tm, tf=256):
    slots, d = xs.shape
    fe = wg.shape[2]
    nf = fe // tf
    nt = slots // tm

    grid_spec = pltpu.PrefetchScalarGridSpec(
        num_scalar_prefetch=3,
        grid=(nt, nf),
        in_specs=[
            pl.BlockSpec((tm, d), lambda i, f, te, tb, nv: (tb[i], 0)),
            pl.BlockSpec(memory_space=pl.ANY), pl.BlockSpec(memory_space=pl.ANY),
            pl.BlockSpec(memory_space=pl.ANY),
        ],
        out_specs=pl.BlockSpec((tm, d), lambda i, f, te, tb, nv: (i, 0)),
        scratch_shapes=[pltpu.VMEM((tm, d), BF16), pltpu.VMEM((W_SLOTS, d, tf), F32),
                        pltpu.VMEM((W_SLOTS, d, tf), F32), pltpu.VMEM((W_SLOTS, tf, d), F32),
                        pltpu.SemaphoreType.DMA((3, W_SLOTS))],
    )
    return pl.pallas_call(
        _expert_kernel,
        grid_spec=grid_spec,
        out_shape=jax.ShapeDtypeStruct((slots, d), F32),
        compiler_params=_params(("arbitrary", "arbitrary"), 60),
        name="moe_experts",
    )(tile_expert, tile_block, tile_rows, xs, wg, wu, wd)


def _dispatch_kernel(pos_ref, trows_ref, h_ref, xs_hbm, zbuf, sem, zsem):
    tmd = h_ref.shape[0]
    tm = zbuf.shape[0]

    @pl.when(pl.program_id(0) == 0)
    def _():
        zbuf[...] = jnp.zeros_like(zbuf)

        def tile_fill(j):
            return pltpu.make_async_copy(zbuf, xs_hbm.at[pl.ds(j * tm, tm)], zsem)

        def fill(j):
            @pl.when(trows_ref[j] < tm)
            def _():
                tile_fill(j).start()

        def fill_wait(j):
            @pl.when(trows_ref[j] < tm)
            def _():
                tile_fill(j).wait()

        _for_rows(trows_ref.shape[0], fill)
        _for_rows(trows_ref.shape[0], fill_wait)

    for tok in range(tmd):
        for k in range(2):
            _row_copy(h_ref, tok, xs_hbm, pos_ref[2 * tok + k], sem).start()
    for _ in range(2):
        pltpu.make_async_copy(h_ref, xs_hbm.at[pl.ds(0, tmd)], sem).wait()


def moe_dispatch(hn, pos, tile_rows, *, tm, tmd=512):
    t, d = hn.shape
    tmd = min(tmd, t)
    return pl.pallas_call(
        _dispatch_kernel,
        grid=(t // tmd,),
        in_specs=[pl.BlockSpec((2 * tmd,), lambda i: (i,), memory_space=pltpu.SMEM),
                  pl.BlockSpec(memory_space=pltpu.SMEM),
                  pl.BlockSpec((tmd, d), lambda i: (i, 0))],
        out_specs=pl.BlockSpec(memory_space=pl.ANY),
        out_shape=jax.ShapeDtypeStruct((tile_rows.shape[0] * tm, d), hn.dtype),
        scratch_shapes=[pltpu.VMEM((tm, d), hn.dtype), pltpu.SemaphoreType.DMA(()),
                        pltpu.SemaphoreType.DMA(())],
        compiler_params=_params(("arbitrary",), 32),
        name="moe_dispatch",
    )(pos, tile_rows, hn)


def _combine_kernel(pos_ref, posn_ref, route_ref, x_ref, g_ref, ys_hbm, o_ref, buf, sem):
    i = pl.program_id(0)
    n = pl.num_programs(0)
    tmc = x_ref.shape[0]

    def gather(pref, slot):
        for tok in range(tmc):
            for k in range(2):
                _row_copy(ys_hbm, pref[2 * tok + k], buf.at[slot, k], tok, sem.at[slot]).start()

    @pl.when(i == 0)
    def _():
        gather(pos_ref, 0)

    @pl.when(i + 1 < n)
    def _():
        gather(posn_ref, (i + 1) % 2)

    cur = i % 2
    for k in range(2):
        pltpu.make_async_copy(ys_hbm.at[pl.ds(0, tmc)], buf.at[cur, k], sem.at[cur]).wait()
    xn = x_ref[...] + (route_ref[:, 4:5] * buf[cur, 0] + route_ref[:, 5:6] * buf[cur, 1])
    o_ref[...] = _rms(xn, g_ref[...]).astype(o_ref.dtype)


def moe_combine(ys, pos, route, x, g, *, tmc=256):
    t, d = x.shape
    tmc = min(tmc, t)
    n = t // tmc
    rowb = pl.BlockSpec((tmc, d), lambda i: (i, 0))
    cur = pl.BlockSpec((2 * tmc,), lambda i: (i,), memory_space=pltpu.SMEM)
    nxt = pl.BlockSpec((2 * tmc,), lambda i: (jnp.minimum(i + 1, n - 1),), memory_space=pltpu.SMEM)
    return pl.pallas_call(
        _combine_kernel,
        grid=(n,),
        in_specs=[cur, nxt,
                  pl.BlockSpec((tmc, LANES), lambda i: (i, 0)), rowb, pl.BlockSpec((1, d), lambda i: (0, 0)),
                  pl.BlockSpec(memory_space=pl.ANY)],
        out_specs=rowb,
        out_shape=jax.ShapeDtypeStruct((t, d), F32),
        scratch_shapes=[pltpu.VMEM((2, 2, tmc, d), F32), pltpu.SemaphoreType.DMA((2,))],
        compiler_params=_params(("arbitrary",), 48),
        name="moe_combine",
    )(pos, pos, route, x, g.reshape(1, d), ys)


def _tile_schedule(counts, tm, nt):
    cnt = counts.astype(jnp.int32)
    ntile = (cnt + tm - 1) // tm
    tend = jnp.cumsum(ntile)
    tstart = tend - ntile
    total = tend[-1]
    tid = jnp.arange(nt, dtype=jnp.int32)
    active = tid < total
    tclamp = jnp.minimum(tid, total - 1)
    te = jnp.sum((tclamp[:, None] >= tend[None, :]).astype(jnp.int32), axis=1)
    rows = jnp.clip(cnt[te] - (tclamp - tstart[te]) * tm, 0, tm)
    return tstart * tm, te, tclamp, jnp.where(active, rows, 0)


def moe_block(hn, route, counts, x, g, wg, wu, wd, *, tm=1024):
    t, d = hn.shape
    nt = -(-2 * t // tm) + N_EXPERTS
    off, te, tblk, trows = _tile_schedule(counts[0, :N_EXPERTS], tm, nt)
    pos = (off[route[:, 0:2].astype(jnp.int32)] + route[:, 2:4].astype(jnp.int32)).reshape(-1)
    xs = moe_dispatch(hn, pos, trows, tm=tm)
    ys = moe_experts(xs, wg, wu, wd, te, tblk, trows, tm=tm)
    return moe_combine(ys, pos, route, x, g)


def _block_diag(w):
    h, n, _ = w.shape
    eye = jnp.eye(h, dtype=w.dtype)
    return (eye[:, None, :, None] * w[:, :, None, :]).reshape(h * n, h * n)


def kernel(x, attn_norm, w_in, attn_sinks, conv_w, lru_conv_w, lru_conv_b, lru_wa, lru_ba, lru_wx,
           lru_bx, lru_lambda, mix_norm, w_out, ffn_norm, dense_w_gate, dense_w_up, dense_w_down,
           router_w, expert_w_gate, expert_w_up, expert_w_down, final_norm):
    b, s, d = x.shape
    depth = w_in.shape[0]
    assert depth == 2 and d == D_MODEL
    xf = x.reshape(b * s, d)
    out = None
    for layer in range(depth):
        proj = inproj(xf, attn_norm[layer], w_in[layer].astype(BF16))
        y = mixer(proj, attn_sinks[layer], conv_w[layer], lru_conv_w[layer], lru_conv_b[layer],
                  _block_diag(lru_wa[layer]).astype(BF16), lru_ba[layer],
                  _block_diag(lru_wx[layer]).astype(BF16), lru_bx[layer], lru_lambda[layer],
                  mix_norm[layer], batch=b)
        j = layer // 2
        if layer % 2 == 0:
            xf = outproj(y, w_out[layer].astype(BF16), xf)
            xf = ffn_dense(xf, ffn_norm[layer], dense_w_gate[j], dense_w_up[j], dense_w_down[j])
        else:
            rw = jnp.pad(router_w[j], ((0, 0), (0, LANES - N_EXPERTS)))
            rw_hi = rw.astype(BF16)
            rw_lo = (rw - rw_hi.astype(F32)).astype(BF16)
            rw2 = jnp.concatenate([jnp.concatenate([rw_hi, rw_lo], axis=1),
                                   jnp.concatenate([rw_hi, jnp.zeros_like(rw_lo)], axis=1)], axis=0)
            xf, hn, route, counts = outproj_route(y, w_out[layer].astype(BF16), xf, ffn_norm[layer], rw2)
            out = moe_block(hn, route, counts, xf, final_norm, expert_w_gate[j], expert_w_up[j],
                            expert_w_down[j])
    return out.reshape(b, s, d)
```

```python
import functools

import jax
import jax.numpy as jnp
from jax import lax
from jax.experimental import pallas as pl
from jax.experimental.pallas import tpu as pltpu

F32 = jnp.float32
BF16 = jnp.bfloat16

D_MODEL = 2048
HEAD_DIM = 64
D_ATTN = 1024
N_HEADS = 16
N_KV_HEADS = 2
GROUP = N_HEADS // N_KV_HEADS
BLOCK = 128
D_KV = N_KV_HEADS * HEAD_DIM
D_CONV = 512
D_LRU = 512
LRU_C = 8.0
LOG2E = 1.4426950408889634
D_IN = 3840
N_EXPERTS = 8
EPS = 1e-6
LANES = 128
SUBLANES = 8

O_Q, O_K, O_V = 0, 1024, 1152
O_CB, O_CC, O_CX = 1280, 1792, 2304
O_LX, O_LG = 2816, 3328

MIB = 2 ** 20


def _params(sem, vmem_mib):
    return pltpu.CompilerParams(dimension_semantics=sem, vmem_limit_bytes=vmem_mib * MIB)


def _rms(x, g):
    return x * lax.rsqrt(jnp.mean(x * x, axis=-1, keepdims=True) + EPS) * g


def _inproj_kernel(x_ref, g_ref, w_ref, o_ref, hb_ref):
    @pl.when(pl.program_id(1) == 0)
    def _():
        hb_ref[...] = _rms(x_ref[...], g_ref[...]).astype(BF16)

    o_ref[...] = jnp.dot(hb_ref[...], w_ref[...], preferred_element_type=F32)


def inproj(x, g, w, *, tm=1024, tn=1280):
    t, d = x.shape
    n = w.shape[1]
    tm = min(tm, t)
    return pl.pallas_call(
        _inproj_kernel,
        grid=(t // tm, n // tn),
        in_specs=[pl.BlockSpec((tm, d), lambda i, j: (i, 0)), pl.BlockSpec((1, d), lambda i, j: (0, 0)),
                  pl.BlockSpec((d, tn), lambda i, j: (0, j))],
        out_specs=pl.BlockSpec((tm, tn), lambda i, j: (i, j)),
        out_shape=jax.ShapeDtypeStruct((t, n), F32),
        scratch_shapes=[pltpu.VMEM((tm, d), BF16)],
        compiler_params=_params(("arbitrary", "arbitrary"), 56),
        name="inproj",
    )(x, g.reshape(1, d), w)


def _shift_rows(u, halo, k):
    r = pltpu.roll(u, k, axis=0)
    hr = pltpu.roll(halo, k, axis=0)
    row = lax.broadcasted_iota(jnp.int32, hr.shape, 0)
    top = jnp.where(row < k, hr, r[:SUBLANES])
    return jnp.concatenate([top, r[SUBLANES:]], axis=0)


def _linear_scan(a, b, h0):
    n, c = a.shape
    g = n // SUBLANES
    a = a.reshape(g, SUBLANES, c)
    b = b.reshape(g, SUBLANES, c)
    sub = lax.broadcasted_iota(jnp.int32, a.shape, 1)
    s = 1
    while s < SUBLANES:
        a_sh = pltpu.roll(a, s, axis=1)
        b_sh = pltpu.roll(b, s, axis=1)
        valid = sub >= s
        b = jnp.where(valid, a * b_sh, 0.0) + b
        a = jnp.where(valid, a * a_sh, a)
        s *= 2
    hs = []
    carry = h0
    for i in range(g):
        h = a[i] * carry + b[i]
        carry = h[SUBLANES - 1:SUBLANES, :]
        hs.append(h)
    return jnp.concatenate(hs, axis=0), carry


def _half_variants(x):
    lane = lax.broadcasted_iota(jnp.int32, x.shape, 1)
    lo = lane < HEAD_DIM
    sw = pltpu.roll(x, HEAD_DIM, axis=1)
    z = jnp.zeros_like(x)
    return [jnp.where(lo, x, z).astype(BF16), jnp.where(lo, z, sw).astype(BF16),
            jnp.where(lo, sw, z).astype(BF16), jnp.where(lo, z, x).astype(BF16)]


def _mixer_kernel(sinks_ref, proj_ref, convw_ref, lcw_ref, lcb_ref, wa_ref, ba_ref, wx_ref, bx_ref,
                  lam_ref, gn_ref, y_ref, kprev, vprev, uhalo, lxhalo, hcarry, yatt):
    tb = pl.program_id(1)
    tt = proj_ref.shape[0]

    @pl.when(tb == 0)
    def _():
        kprev[...] = jnp.zeros_like(kprev)
        vprev[...] = jnp.zeros_like(vprev)
        uhalo[...] = jnp.zeros_like(uhalo)
        lxhalo[...] = jnp.zeros_like(lxhalo)
        hcarry[...] = jnp.zeros_like(hcarry)

    rows4 = GROUP // 2 * BLOCK
    row = lax.broadcasted_iota(jnp.int32, (rows4, LANES), 0)
    col = lax.broadcasted_iota(jnp.int32, (rows4, LANES), 1)
    in_cur = col <= (row & (BLOCK - 1))
    even = col < HEAD_DIM
    even1 = lax.broadcasted_iota(jnp.int32, (BLOCK, LANES), 1) < HEAD_DIM
    ones_even = jnp.where(even1, 1.0, 0.0).astype(BF16)
    ones_odd = jnp.where(even1, 0.0, 1.0).astype(BF16)
    den_cols = jnp.concatenate([ones_even, ones_even, ones_odd, ones_odd], axis=0)
    first_bias = jnp.where(tb > 0, 0.0, -jnp.inf).astype(F32)
    for sb in range(tt // BLOCK):
        rows = slice(sb * BLOCK, (sb + 1) * BLOCK)
        kc = _half_variants(proj_ref[rows, O_K:O_K + D_KV])
        vc = _half_variants(proj_ref[rows, O_V:O_V + D_KV])
        if sb == 0:
            kp = [kprev[i] for i in range(4)]
            vp = [vprev[i] for i in range(4)]
        for kvh in range(N_KV_HEADS):
            pairs = range(kvh * GROUP // 2, (kvh + 1) * GROUP // 2)
            lo, hi = 2 * kvh, 2 * kvh + 1
            q4 = jnp.concatenate([proj_ref[rows, p * LANES:(p + 1) * LANES] for p in pairs], axis=0)
            q4 = (q4 * (HEAD_DIM ** -0.5 * LOG2E)).astype(BF16)
            kk = jnp.concatenate([kc[lo], kp[lo], kc[hi], kp[hi]], axis=0)
            s = lax.dot_general(q4, kk, (((1,), (1,)), ((), ())), preferred_element_type=F32)
            pps, sink_terms = [], []
            for par in range(2):
                c0 = 2 * BLOCK * par
                s_prev = s[:, c0 + BLOCK:c0 + 2 * BLOCK]
                if sb == 0:
                    s_prev = s_prev + first_bias
                ssel = jnp.where(in_cur, s[:, c0:c0 + BLOCK], s_prev)
                sk = [sinks_ref[2 * p + par] * LOG2E for p in pairs]
                sink = jnp.where(row < BLOCK, sk[0], jnp.where(row < 2 * BLOCK, sk[1],
                                 jnp.where(row < 3 * BLOCK, sk[2], sk[3])))
                m = jnp.maximum(jnp.max(ssel, axis=-1, keepdims=True), sink)
                p2 = jnp.exp2(ssel - m)
                pps += [jnp.where(in_cur, p2, 0.0).astype(BF16), jnp.where(in_cur, 0.0, p2).astype(BF16)]
                sink_terms.append(jnp.exp2(sink - m))
            vv = jnp.concatenate([vc[lo], vp[lo], vc[hi], vp[hi]], axis=0)
            r = jnp.dot(jnp.concatenate(pps, axis=1), jnp.concatenate([vv, den_cols], axis=1),
                        preferred_element_type=F32)
            den = r[:, LANES:] + jnp.where(even, sink_terms[0], sink_terms[1])
            o4 = r[:, :LANES] / den
            for idx, p in enumerate(pairs):
                yatt[rows, p * LANES:(p + 1) * LANES] = o4[idx * BLOCK:(idx + 1) * BLOCK]
        kp, vp = kc, vc
    for i in range(4):
        kprev[i] = kp[i]
        vprev[i] = vp[i]
    ya = yatt[...]
    y_ref[:, 0:D_ATTN] = _rms(ya, gn_ref[:, 0:D_ATTN]).astype(y_ref.dtype)

    u = proj_ref[:, O_CC:O_CC + D_CONV] * proj_ref[:, O_CX:O_CX + D_CONV]
    uh = uhalo[...]
    conv = (convw_ref[0:1, :] * u + convw_ref[1:2, :] * _shift_rows(u, uh, 1)
            + convw_ref[2:3, :] * _shift_rows(u, uh, 2))
    uhalo[...] = u[tt - SUBLANES:, :]
    yc = proj_ref[:, O_CB:O_CB + D_CONV] * conv
    y_ref[:, D_ATTN:D_ATTN + D_CONV] = _rms(yc, gn_ref[:, D_ATTN:D_ATTN + D_CONV]).astype(y_ref.dtype)

    lx = proj_ref[:, O_LX:O_LX + D_LRU]
    lh = lxhalo[...]
    xc = lcw_ref[0:1, :] * lx + lcb_ref[...]
    for k in range(1, 4):
        xc = xc + lcw_ref[k:k + 1, :] * _shift_rows(lx, lh, k)
    lxhalo[...] = lx[tt - SUBLANES:, :]
    xcb = xc.astype(BF16)
    r = jax.nn.sigmoid(jnp.dot(xcb, wa_ref[...], preferred_element_type=F32) + ba_ref[...])
    ig = jax.nn.sigmoid(jnp.dot(xcb, wx_ref[...], preferred_element_type=F32) + bx_ref[...])
    log_a = (-LRU_C) * r * jax.nn.softplus(-lam_ref[...])
    a = jnp.exp(log_a)
    th = jnp.tanh(log_a)
    bu = jnp.sqrt(-2.0 * th / (1.0 - th)) * (ig * xc)
    h, h_last = _linear_scan(a, bu, hcarry[0:1, :])
    hcarry[...] = jnp.broadcast_to(h_last, hcarry.shape)
    yl = h * jax.nn.gelu(proj_ref[:, O_LG:O_LG + D_LRU])
    y_ref[:, D_ATTN + D_CONV:] = _rms(yl, gn_ref[:, D_ATTN + D_CONV:]).astype(y_ref.dtype)


def mixer(proj, sinks, conv_w, lru_conv_w, lru_conv_b, wa_bd, ba, wx_bd, bx, lam, mix_norm,
          *, batch, tt=512):
    t = proj.shape[0]
    s = t // batch
    tt = min(tt, s)
    nt = s // tt
    row_map = lambda b, i: (b * nt + i, 0)
    full = lambda shape: pl.BlockSpec(shape, lambda b, i: (0,) * len(shape))
    return pl.pallas_call(
        _mixer_kernel,
        grid=(batch, nt),
        in_specs=[
            pl.BlockSpec(memory_space=pltpu.SMEM),
            pl.BlockSpec((tt, D_IN), row_map),
            full((3, D_CONV)), full((4, D_LRU)), full((1, D_LRU)),
            full((D_LRU, D_LRU)), full((1, D_LRU)), full((D_LRU, D_LRU)), full((1, D_LRU)),
            full((1, D_LRU)), full((1, D_MODEL)),
        ],
        out_specs=pl.BlockSpec((tt, D_MODEL), row_map),
        out_shape=jax.ShapeDtypeStruct((t, D_MODEL), BF16),
        scratch_shapes=[
            pltpu.VMEM((4, BLOCK, LANES), BF16), pltpu.VMEM((4, BLOCK, LANES), BF16),
            pltpu.VMEM((SUBLANES, D_CONV), F32), pltpu.VMEM((SUBLANES, D_LRU), F32),
            pltpu.VMEM((SUBLANES, D_LRU), F32), pltpu.VMEM((tt, D_ATTN), F32),
        ],
        compiler_params=_params(("arbitrary", "arbitrary"), 48),
        name="mixer",
    )(sinks, proj, conv_w, lru_conv_w, lru_conv_b.reshape(1, -1), wa_bd, ba.reshape(1, -1),
      wx_bd, bx.reshape(1, -1), lam.reshape(1, -1), mix_norm.reshape(1, -1))


def _route(hn, rw_ref, carry):
    tm = hn.shape[0]
    h_hi = hn.astype(BF16)
    h_lo = (hn - h_hi.astype(F32)).astype(BF16)
    res = jnp.dot(jnp.concatenate([h_hi, h_lo], axis=1), rw_ref[...], preferred_element_type=F32)
    logits = res[:, :LANES] + res[:, LANES:]
    lane = lax.broadcasted_iota(jnp.int32, (tm, LANES), 1).astype(F32)
    neg = -jnp.inf
    l1 = jnp.where(lane < N_EXPERTS, logits, neg)
    m1 = jnp.max(l1, axis=-1, keepdims=True)
    i1 = jnp.min(jnp.where(l1 == m1, lane, float(LANES)), axis=-1, keepdims=True)
    l2 = jnp.where(lane == i1, neg, l1)
    m2 = jnp.max(l2, axis=-1, keepdims=True)
    i2 = jnp.min(jnp.where(l2 == m2, lane, float(LANES)), axis=-1, keepdims=True)
    e21 = jnp.exp(m2 - m1)
    g1 = 1.0 / (1.0 + e21)
    g2 = e21 / (1.0 + e21)
    sel1 = lane == i1
    sel2 = lane == i2
    onehot = jnp.where(sel1 | sel2, 1.0, 0.0).astype(BF16)
    rr = lax.broadcasted_iota(jnp.int32, (tm, tm), 0)
    cc = lax.broadcasted_iota(jnp.int32, (tm, tm), 1)
    tri = jnp.where(rr >= cc, 1.0, 0.0).astype(BF16)
    counts = jnp.dot(tri, onehot, preferred_element_type=F32) + carry
    r1 = jnp.sum(jnp.where(sel1, counts, 0.0), axis=-1, keepdims=True) - 1.0
    r2 = jnp.sum(jnp.where(sel2, counts, 0.0), axis=-1, keepdims=True) - 1.0
    packed = jnp.where(lane == 0.0, i1, jnp.where(lane == 1.0, i2, jnp.where(lane == 2.0, r1,
             jnp.where(lane == 3.0, r2, jnp.where(lane == 4.0, g1, jnp.where(lane == 5.0, g2, 0.0))))))
    return packed, counts[tm - 1:tm, :]


def _outproj_kernel(y_ref, w_ref, x_ref, xo_ref):
    xo_ref[...] = x_ref[...] + jnp.dot(y_ref[...], w_ref[...], preferred_element_type=F32)


def _outproj_route_kernel(y_ref, w_ref, x_ref, g_ref, rw_ref, xo_ref, ho_ref, route_ref, cnt_ref,
                          carry, hprev):
    i = pl.program_id(0)

    @pl.when(i == 0)
    def _():
        carry[...] = jnp.zeros_like(carry)
        hprev[...] = jnp.zeros_like(hprev)

    packed, last = _route(hprev[...], rw_ref, carry[0:1, :])
    route_ref[...] = packed
    counts = jnp.where(i > 0, jnp.broadcast_to(last, carry.shape), carry[...])
    carry[...] = counts
    cnt_ref[...] = counts
    xn = x_ref[...] + jnp.dot(y_ref[...], w_ref[...], preferred_element_type=F32)
    xo_ref[...] = xn
    hn = _rms(xn, g_ref[...])
    ho_ref[...] = hn
    hprev[...] = hn


def outproj(y, w, x, *, tm=512):
    t, d = x.shape
    tm = min(tm, t)
    rowb = pl.BlockSpec((tm, d), lambda i: (i, 0))
    return pl.pallas_call(
        _outproj_kernel,
        grid=(t // tm,),
        in_specs=[rowb, pl.BlockSpec((d, d), lambda i: (0, 0)), rowb],
        out_specs=rowb,
        out_shape=jax.ShapeDtypeStruct((t, d), F32),
        compiler_params=_params(("arbitrary",), 48),
        name="outproj",
    )(y, w, x)


def outproj_route(y, w, x, g, rw2, *, tm=512):
    t, d = x.shape
    tm = min(tm, t)
    n = t // tm
    rowb = pl.BlockSpec((tm, d), lambda i: (jnp.minimum(i, n - 1), 0))
    return pl.pallas_call(
        _outproj_route_kernel,
        grid=(n + 1,),
        in_specs=[rowb, pl.BlockSpec((d, d), lambda i: (0, 0)), rowb, pl.BlockSpec((1, d), lambda i: (0, 0)),
                  pl.BlockSpec((2 * d, 2 * LANES), lambda i: (0, 0))],
        out_specs=[rowb, rowb, pl.BlockSpec((tm, LANES), lambda i: (jnp.maximum(i - 1, 0), 0)),
                   pl.BlockSpec((SUBLANES, LANES), lambda i: (0, 0))],
        out_shape=[jax.ShapeDtypeStruct((t, d), F32), jax.ShapeDtypeStruct((t, d), F32),
                   jax.ShapeDtypeStruct((t, LANES), F32), jax.ShapeDtypeStruct((SUBLANES, LANES), F32)],
        scratch_shapes=[pltpu.VMEM((SUBLANES, LANES), F32), pltpu.VMEM((tm, d), F32)],
        compiler_params=_params(("arbitrary",), 56),
        name="outproj_route",
    )(y, w, x, g.reshape(1, d), rw2)


def _swiglu_step(xb, wg, wu, wd):
    g = jnp.dot(xb, wg, preferred_element_type=F32)
    u = jnp.dot(xb, wu, preferred_element_type=F32)
    a = (g * jax.nn.sigmoid(g) * u).astype(BF16)
    return jnp.dot(a, wd, preferred_element_type=F32)


W_SLOTS = 3


def _weight_copies(wg_hbm, wu_hbm, wd_hbm, f, tf, wgr, wur, wdr, sem, slot, expert=None):
    cols = pl.ds(pl.multiple_of(f * tf, tf), tf)
    if expert is not None:
        wg_hbm, wu_hbm, wd_hbm = wg_hbm.at[expert], wu_hbm.at[expert], wd_hbm.at[expert]
    return (pltpu.make_async_copy(wg_hbm.at[:, cols], wgr.at[slot], sem.at[0, slot]),
            pltpu.make_async_copy(wu_hbm.at[:, cols], wur.at[slot], sem.at[1, slot]),
            pltpu.make_async_copy(wd_hbm.at[cols, :], wdr.at[slot], sem.at[2, slot]))


def _ffn_kernel(x_ref, g_ref, wg_hbm, wu_hbm, wd_hbm, xo_ref, hb_ref, wgr, wur, wdr, sem):
    i = pl.program_id(0)
    f = pl.program_id(1)
    nf = pl.num_programs(1)
    tf = wgr.shape[2]
    step = i * nf + f
    last = pl.num_programs(0) * nf - 1

    def fetch(s):
        for cp in _weight_copies(wg_hbm, wu_hbm, wd_hbm, lax.rem(s, nf), tf, wgr, wur, wdr, sem,
                                 lax.rem(s, W_SLOTS)):
            cp.start()

    @pl.when(step == 0)
    def _():
        fetch(step)
        fetch(step + 1)

    @pl.when(step + 2 <= last)
    def _():
        fetch(step + 2)

    @pl.when(f == 0)
    def _():
        x = x_ref[...]
        hb_ref[...] = _rms(x, g_ref[...]).astype(BF16)
        xo_ref[...] = x

    slot = lax.rem(step, W_SLOTS)
    for cp in _weight_copies(wg_hbm, wu_hbm, wd_hbm, f, tf, wgr, wur, wdr, sem, slot):
        cp.wait()
    xo_ref[...] += _swiglu_step(hb_ref[...], wgr[slot].astype(BF16), wur[slot].astype(BF16),
                                wdr[slot].astype(BF16))


def ffn_dense(x, g, wg, wu, wd, *, tm=1024, tf=256):
    t, d = x.shape
    ff = wg.shape[1]
    tm = min(tm, t)
    rowb = pl.BlockSpec((tm, d), lambda i, f: (i, 0))
    return pl.pallas_call(
        _ffn_kernel,
        grid=(t // tm, ff // tf),
        in_specs=[rowb, pl.BlockSpec((1, d), lambda i, f: (0, 0)),
                  pl.BlockSpec(memory_space=pl.ANY), pl.BlockSpec(memory_space=pl.ANY),
                  pl.BlockSpec(memory_space=pl.ANY)],
        out_specs=rowb,
        out_shape=jax.ShapeDtypeStruct((t, d), F32),
        scratch_shapes=[pltpu.VMEM((tm, d), BF16), pltpu.VMEM((W_SLOTS, d, tf), F32),
                        pltpu.VMEM((W_SLOTS, d, tf), F32), pltpu.VMEM((W_SLOTS, tf, d), F32),
                        pltpu.SemaphoreType.DMA((3, W_SLOTS))],
        compiler_params=_params(("arbitrary", "arbitrary"), 60),
        name="ffn_dense",
    )(x, g.reshape(1, d), wg, wu, wd)


def _row_copy(src, src_row, dst, dst_row, sem):
    return pltpu.make_async_copy(src.at[pl.ds(src_row, 1)], dst.at[pl.ds(dst_row, 1)], sem)


def _for_rows(n, fn):
    def body(r, c):
        fn(r)
        return c

    lax.fori_loop(0, n, body, 0)


M_ALWAYS = 512
M_CHUNK = 256


def _expert_kernel(te_ref, tb_ref, nv_ref, xs_ref, wg_hbm, wu_hbm, wd_hbm, ys_ref, xb, wgr, wur, wdr, sem):
    i = pl.program_id(0)
    f = pl.program_id(1)
    nt = pl.num_programs(0)
    nf = pl.num_programs(1)
    tm = xb.shape[0]
    tf = wgr.shape[2]
    nv = nv_ref[i]
    step = i * nf + f

    def copies(tile, fs, s):
        return _weight_copies(wg_hbm, wu_hbm, wd_hbm, fs, tf, wgr, wur, wdr, sem, lax.rem(s, W_SLOTS),
                              expert=te_ref[tile])

    def fetch_ahead(k):
        over = f + k >= nf
        tile = jnp.minimum(i + over.astype(jnp.int32), nt - 1)
        fs = f + k - nf * over.astype(jnp.int32)

        @pl.when(jnp.logical_not(over) | ((i + 1 < nt) & (nv_ref[tile] > 0)))
        def _():
            for cp in copies(tile, fs, step + k):
                cp.start()

    @pl.when(nv > 0)
    def _():
        @pl.when(step == 0)
        def _():
            fetch_ahead(0)
            fetch_ahead(1)

        fetch_ahead(2)

        @pl.when(f == 0)
        def _():
            xb[...] = xs_ref[...].astype(BF16)
            ys_ref[...] = jnp.zeros_like(ys_ref)

        slot = lax.rem(step, W_SLOTS)
        for cp in copies(i, f, step):
            cp.wait()
        m0 = min(M_ALWAYS, tm)

        @pl.when(nv > tm - M_CHUNK)
        def _():
            ys_ref[...] += _swiglu_step(xb[...], wgr[slot].astype(BF16), wur[slot].astype(BF16),
                                        wdr[slot].astype(BF16))

        @pl.when(nv <= tm - M_CHUNK)
        def _():
            ys_ref[0:m0, :] += _swiglu_step(xb[0:m0, :], wgr[slot].astype(BF16), wur[slot].astype(BF16),
                                            wdr[slot].astype(BF16))
            for c in range(m0, tm - M_CHUNK, M_CHUNK):
                @pl.when(c < nv)
                def _():
                    rows = pl.ds(c, M_CHUNK)
                    ys_ref[rows, :] += _swiglu_step(xb[rows, :], wgr[slot].astype(BF16),
                                                    wur[slot].astype(BF16), wdr[slot].astype(BF16))

    @pl.when((nv == 0) & (f == nf - 1))
    def _():
        ys_ref[...] = jnp.zeros_like(ys_ref)


def moe_experts(xs, wg, wu, wd, tile_expert, tile_block, tile_rows, *, tm, tf=256):
    slots, d = xs.shape
    fe = wg.shape[2]
    nf = fe // tf
    nt = slots // tm

    grid_spec = pltpu.PrefetchScalarGridSpec(
        num_scalar_prefetch=3,
        grid=(nt, nf),
        in_specs=[
            pl.BlockSpec((tm, d), lambda i, f, te, tb, nv: (tb[i], 0)),
            pl.BlockSpec(memory_space=pl.ANY), pl.BlockSpec(memory_space=pl.ANY),
            pl.BlockSpec(memory_space=pl.ANY),
        ],
        out_specs=pl.BlockSpec((tm, d), lambda i, f, te, tb, nv: (i, 0)),
        scratch_shapes=[pltpu.VMEM((tm, d), BF16), pltpu.VMEM((W_SLOTS, d, tf), F32),
                        pltpu.VMEM((W_SLOTS, d, tf), F32), pltpu.VMEM((W_SLOTS, tf, d), F32),
                        pltpu.SemaphoreType.DMA((3, W_SLOTS))],
    )
    return pl.pallas_call(
        _expert_kernel,
        grid_spec=grid_spec,
        out_shape=jax.ShapeDtypeStruct((slots, d), F32),
        compiler_params=_params(("arbitrary", "arbitrary"), 60),
        name="moe_experts",
    )(tile_expert, tile_block, tile_rows, xs, wg, wu, wd)


def _dispatch_kernel(pos_ref, trows_ref, h_ref, xs_hbm, zbuf, sem, zsem):
    tmd = h_ref.shape[0]
    tm = zbuf.shape[0]

    @pl.when(pl.program_id(0) == 0)
    def _():
        zbuf[...] = jnp.zeros_like(zbuf)

        def tile_fill(j):
            return pltpu.make_async_copy(zbuf, xs_hbm.at[pl.ds(j * tm, tm)], zsem)

        def fill(j):
            @pl.when(trows_ref[j] < tm)
            def _():
                tile_fill(j).start()

        def fill_wait(j):
            @pl.when(trows_ref[j] < tm)
            def _():
                tile_fill(j).wait()

        _for_rows(trows_ref.shape[0], fill)
        _for_rows(trows_ref.shape[0], fill_wait)

    for tok in range(tmd):
        for k in range(2):
            _row_copy(h_ref, tok, xs_hbm, pos_ref[2 * tok + k], sem).start(priority=k)
    for _ in range(2):
        pltpu.make_async_copy(h_ref, xs_hbm.at[pl.ds(0, tmd)], sem).wait()


def moe_dispatch(hn, pos, tile_rows, *, tm, tmd=512):
    t, d = hn.shape
    tmd = min(tmd, t)
    return pl.pallas_call(
        _dispatch_kernel,
        grid=(t // tmd,),
        in_specs=[pl.BlockSpec((2 * tmd,), lambda i: (i,), memory_space=pltpu.SMEM),
                  pl.BlockSpec(memory_space=pltpu.SMEM),
                  pl.BlockSpec((tmd, d), lambda i: (i, 0))],
        out_specs=pl.BlockSpec(memory_space=pl.ANY),
        out_shape=jax.ShapeDtypeStruct((tile_rows.shape[0] * tm, d), hn.dtype),
        scratch_shapes=[pltpu.VMEM((tm, d), hn.dtype), pltpu.SemaphoreType.DMA(()),
                        pltpu.SemaphoreType.DMA(())],
        compiler_params=_params(("arbitrary",), 32),
        name="moe_dispatch",
    )(pos, tile_rows, hn)


def _combine_kernel(pos_ref, posn_ref, route_ref, x_ref, g_ref, ys_hbm, o_ref, buf, sem):
    i = pl.program_id(0)
    n = pl.num_programs(0)
    tmc = x_ref.shape[0]

    def gather(pref, slot):
        for tok in range(tmc):
            for k in range(2):
                _row_copy(ys_hbm, pref[2 * tok + k], buf.at[slot, k], tok, sem.at[slot]).start(priority=k)

    @pl.when(i == 0)
    def _():
        gather(pos_ref, 0)

    @pl.when(i + 1 < n)
    def _():
        gather(posn_ref, (i + 1) % 2)

    cur = i % 2
    for k in range(2):
        pltpu.make_async_copy(ys_hbm.at[pl.ds(0, tmc)], buf.at[cur, k], sem.at[cur]).wait()
    xn = x_ref[...] + (route_ref[:, 4:5] * buf[cur, 0] + route_ref[:, 5:6] * buf[cur, 1])
    o_ref[...] = _rms(xn, g_ref[...]).astype(o_ref.dtype)


def moe_combine(ys, pos, route, x, g, *, tmc=256):
    t, d = x.shape
    tmc = min(tmc, t)
    n = t // tmc
    rowb = pl.BlockSpec((tmc, d), lambda i: (i, 0))
    cur = pl.BlockSpec((2 * tmc,), lambda i: (i,), memory_space=pltpu.SMEM)
    nxt = pl.BlockSpec((2 * tmc,), lambda i: (jnp.minimum(i + 1, n - 1),), memory_space=pltpu.SMEM)
    return pl.pallas_call(
        _combine_kernel,
        grid=(n,),
        in_specs=[cur, nxt,
                  pl.BlockSpec((tmc, LANES), lambda i: (i, 0)), rowb, pl.BlockSpec((1, d), lambda i: (0, 0)),
                  pl.BlockSpec(memory_space=pl.ANY)],
        out_specs=rowb,
        out_shape=jax.ShapeDtypeStruct((t, d), F32),
        scratch_shapes=[pltpu.VMEM((2, 2, tmc, d), F32), pltpu.SemaphoreType.DMA((2,))],
        compiler_params=_params(("arbitrary",), 48),
        name="moe_combine",
    )(pos, pos, route, x, g.reshape(1, d), ys)


def _tile_schedule(counts, tm, nt):
    cnt = counts.astype(jnp.int32)
    ntile = (cnt + tm - 1) // tm
    tend = jnp.cumsum(ntile)
    tstart = tend - ntile
    total = tend[-1]
    tid = jnp.arange(nt, dtype=jnp.int32)
    active = tid < total
    tclamp = jnp.minimum(tid, total - 1)
    te = jnp.sum((tclamp[:, None] >= tend[None, :]).astype(jnp.int32), axis=1)
    rows = jnp.clip(cnt[te] - (tclamp - tstart[te]) * tm, 0, tm)
    return tstart * tm, te, tclamp, jnp.where(active, rows, 0)


def moe_block(hn, route, counts, x, g, wg, wu, wd, *, tm=1024):
    t, d = hn.shape
    nt = -(-2 * t // tm) + N_EXPERTS
    off, te, tblk, trows = _tile_schedule(counts[0, :N_EXPERTS], tm, nt)
    pos = (off[route[:, 0:2].astype(jnp.int32)] + route[:, 2:4].astype(jnp.int32)).reshape(-1)
    xs = moe_dispatch(hn, pos, trows, tm=tm)
    ys = moe_experts(xs, wg, wu, wd, te, tblk, trows, tm=tm)
    return moe_combine(ys, pos, route, x, g)


def _block_diag(w):
    h, n, _ = w.shape
    eye = jnp.eye(h, dtype=w.dtype)
    return (eye[:, None, :, None] * w[:, :, None, :]).reshape(h * n, h * n)


def kernel(x, attn_norm, w_in, attn_sinks, conv_w, lru_conv_w, lru_conv_b, lru_wa, lru_ba, lru_wx,
           lru_bx, lru_lambda, mix_norm, w_out, ffn_norm, dense_w_gate, dense_w_up, dense_w_down,
           router_w, expert_w_gate, expert_w_up, expert_w_down, final_norm):
    b, s, d = x.shape
    depth = w_in.shape[0]
    assert depth == 2 and d == D_MODEL
    xf = x.reshape(b * s, d)
    out = None
    for layer in range(depth):
        proj = inproj(xf, attn_norm[layer], w_in[layer].astype(BF16))
        y = mixer(proj, attn_sinks[layer], conv_w[layer], lru_conv_w[layer], lru_conv_b[layer],
                  _block_diag(lru_wa[layer]).astype(BF16), lru_ba[layer],
                  _block_diag(lru_wx[layer]).astype(BF16), lru_bx[layer], lru_lambda[layer],
                  mix_norm[layer], batch=b)
        j = layer // 2
        if layer % 2 == 0:
            xf = outproj(y, w_out[layer].astype(BF16), xf)
            xf = ffn_dense(xf, ffn_norm[layer], dense_w_gate[j], dense_w_up[j], dense_w_down[j])
        else:
            rw = jnp.pad(router_w[j], ((0, 0), (0, LANES - N_EXPERTS)))
            rw_hi = rw.astype(BF16)
            rw_lo = (rw - rw_hi.astype(F32)).astype(BF16)
            rw2 = jnp.concatenate([jnp.concatenate([rw_hi, rw_lo], axis=1),
                                   jnp.concatenate([rw_hi, jnp.zeros_like(rw_lo)], axis=1)], axis=0)
            xf, hn, route, counts = outproj_route(y, w_out[layer].astype(BF16), xf, ffn_norm[layer], rw2)
            out = moe_block(hn, route, counts, xf, final_norm, expert_w_gate[j], expert_w_up[j],
                            expert_w_down[j])
    return out.reshape(b, s, d)
```

```python
import jax
import jax.numpy as jnp
from jax import lax
from jax.experimental import pallas as pl
from jax.experimental.pallas import tpu as pltpu

F32 = jnp.float32
BF16 = jnp.bfloat16

D_MODEL = 2048
HEAD_DIM = 64
D_ATTN = 1024
N_HEADS = 16
N_KV_HEADS = 2
GROUP = N_HEADS // N_KV_HEADS
BLOCK = 128
D_KV = N_KV_HEADS * HEAD_DIM
D_CONV = 512
D_LRU = 512
LRU_C = 8.0
LOG2E = 1.4426950408889634
D_IN = 3840
N_EXPERTS = 8
EPS = 1e-6
LANES = 128
SUBLANES = 8

O_Q, O_K, O_V = 0, 1024, 1152
O_CB, O_CC, O_CX = 1280, 1792, 2304
O_LX, O_LG = 2816, 3328

MIB = 2 ** 20

INPROJ_TM, INPROJ_TN, INPROJ_VMEM = 1024, 1280, 56
MIXER_TT, MIXER_VMEM = 512, 48
OUTPROJ_TM, OUTPROJ_VMEM, ROUTE_VMEM = 512, 48, 56
FFN_TM, FFN_TF, FFN_VMEM = 1024, 256, 60
MOE_TM, MOE_TF, MOE_VMEM = 1024, 256, 60
DISPATCH_TM, DISPATCH_VMEM = 1024, 40
COMBINE_TM, COMBINE_VMEM = 256, 48


def _params(sem, vmem_mib):
    return pltpu.CompilerParams(dimension_semantics=sem, vmem_limit_bytes=vmem_mib * MIB)


def _rms(x, g):
    return x * lax.rsqrt(jnp.mean(x * x, axis=-1, keepdims=True) + EPS) * g


def _inproj_kernel(x_ref, g_ref, w_ref, o_ref, hb_ref):
    @pl.when(pl.program_id(1) == 0)
    def _():
        hb_ref[...] = _rms(x_ref[...], g_ref[...]).astype(BF16)

    o_ref[...] = jnp.dot(hb_ref[...], w_ref[...], preferred_element_type=F32)


def inproj(x, g, w, *, tm=INPROJ_TM, tn=INPROJ_TN):
    t, d = x.shape
    n = w.shape[1]
    tm = min(tm, t)
    return pl.pallas_call(
        _inproj_kernel,
        grid=(t // tm, n // tn),
        in_specs=[pl.BlockSpec((tm, d), lambda i, j: (i, 0)), pl.BlockSpec((1, d), lambda i, j: (0, 0)),
                  pl.BlockSpec((d, tn), lambda i, j: (0, j))],
        out_specs=pl.BlockSpec((tm, tn), lambda i, j: (i, j)),
        out_shape=jax.ShapeDtypeStruct((t, n), F32),
        scratch_shapes=[pltpu.VMEM((tm, d), BF16)],
        compiler_params=_params(("arbitrary", "arbitrary"), INPROJ_VMEM),
        name="inproj",
    )(x, g.reshape(1, d), w)


def _shift_rows(u, halo, k):
    r = pltpu.roll(u, k, axis=0)
    hr = pltpu.roll(halo, k, axis=0)
    row = lax.broadcasted_iota(jnp.int32, hr.shape, 0)
    top = jnp.where(row < k, hr, r[:SUBLANES])
    return jnp.concatenate([top, r[SUBLANES:]], axis=0)


def _linear_scan(a, b, h0):
    n, c = a.shape
    g = n // SUBLANES
    a = a.reshape(g, SUBLANES, c)
    b = b.reshape(g, SUBLANES, c)
    sub = lax.broadcasted_iota(jnp.int32, a.shape, 1)
    s = 1
    while s < SUBLANES:
        a_sh = pltpu.roll(a, s, axis=1)
        b_sh = pltpu.roll(b, s, axis=1)
        valid = sub >= s
        b = jnp.where(valid, a * b_sh, 0.0) + b
        a = jnp.where(valid, a * a_sh, a)
        s *= 2
    hs = []
    carry = h0
    for i in range(g):
        h = a[i] * carry + b[i]
        carry = h[SUBLANES - 1:SUBLANES, :]
        hs.append(h)
    return jnp.concatenate(hs, axis=0), carry


def _half_variants(x):
    lane = lax.broadcasted_iota(jnp.int32, x.shape, 1)
    lo = lane < HEAD_DIM
    sw = pltpu.roll(x, HEAD_DIM, axis=1)
    z = jnp.zeros_like(x)
    return [jnp.where(lo, x, z).astype(BF16), jnp.where(lo, z, sw).astype(BF16),
            jnp.where(lo, sw, z).astype(BF16), jnp.where(lo, z, x).astype(BF16)]


def _mixer_kernel(sinks_ref, proj_ref, convw_ref, lcw_ref, lcb_ref, wa_ref, ba_ref, wx_ref, bx_ref,
                  lam_ref, gn_ref, y_ref, kprev, vprev, uhalo, lxhalo, hcarry, yatt):
    tb = pl.program_id(1)
    tt = proj_ref.shape[0]

    @pl.when(tb == 0)
    def _():
        kprev[...] = jnp.zeros_like(kprev)
        vprev[...] = jnp.zeros_like(vprev)
        uhalo[...] = jnp.zeros_like(uhalo)
        lxhalo[...] = jnp.zeros_like(lxhalo)
        hcarry[...] = jnp.zeros_like(hcarry)

    rows4 = GROUP // 2 * BLOCK
    row = lax.broadcasted_iota(jnp.int32, (rows4, LANES), 0)
    col = lax.broadcasted_iota(jnp.int32, (rows4, LANES), 1)
    in_cur = col <= (row & (BLOCK - 1))
    even = col < HEAD_DIM
    even1 = lax.broadcasted_iota(jnp.int32, (BLOCK, LANES), 1) < HEAD_DIM
    ones_even = jnp.where(even1, 1.0, 0.0).astype(BF16)
    ones_odd = jnp.where(even1, 0.0, 1.0).astype(BF16)
    den_cols = jnp.concatenate([ones_even, ones_even, ones_odd, ones_odd], axis=0)
    first_bias = jnp.where(tb > 0, 0.0, -jnp.inf).astype(F32)
    for sb in range(tt // BLOCK):
        rows = slice(sb * BLOCK, (sb + 1) * BLOCK)
        kc = _half_variants(proj_ref[rows, O_K:O_K + D_KV])
        vc = _half_variants(proj_ref[rows, O_V:O_V + D_KV])
        if sb == 0:
            kp = [kprev[i] for i in range(4)]
            vp = [vprev[i] for i in range(4)]
        for kvh in range(N_KV_HEADS):
            pairs = range(kvh * GROUP // 2, (kvh + 1) * GROUP // 2)
            lo, hi = 2 * kvh, 2 * kvh + 1
            q4 = jnp.concatenate([proj_ref[rows, p * LANES:(p + 1) * LANES] for p in pairs], axis=0)
            q4 = (q4 * (HEAD_DIM ** -0.5 * LOG2E)).astype(BF16)
            kk = jnp.concatenate([kc[lo], kp[lo], kc[hi], kp[hi]], axis=0)
            s = lax.dot_general(q4, kk, (((1,), (1,)), ((), ())), preferred_element_type=F32)
            pps, sink_terms = [], []
            for par in range(2):
                c0 = 2 * BLOCK * par
                s_prev = s[:, c0 + BLOCK:c0 + 2 * BLOCK]
                if sb == 0:
                    s_prev = s_prev + first_bias
                ssel = jnp.where(in_cur, s[:, c0:c0 + BLOCK], s_prev)
                sk = [sinks_ref[2 * p + par] * LOG2E for p in pairs]
                sink = jnp.where(row < BLOCK, sk[0], jnp.where(row < 2 * BLOCK, sk[1],
                                 jnp.where(row < 3 * BLOCK, sk[2], sk[3])))
                m = jnp.maximum(jnp.max(ssel, axis=-1, keepdims=True), sink)
                p2 = jnp.exp2(ssel - m)
                pps += [jnp.where(in_cur, p2, 0.0).astype(BF16), jnp.where(in_cur, 0.0, p2).astype(BF16)]
                sink_terms.append(jnp.exp2(sink - m))
            vv = jnp.concatenate([vc[lo], vp[lo], vc[hi], vp[hi]], axis=0)
            r = jnp.dot(jnp.concatenate(pps, axis=1), jnp.concatenate([vv, den_cols], axis=1),
                        preferred_element_type=F32)
            den = r[:, LANES:] + jnp.where(even, sink_terms[0], sink_terms[1])
            o4 = r[:, :LANES] / den
            for idx, p in enumerate(pairs):
                yatt[rows, p * LANES:(p + 1) * LANES] = o4[idx * BLOCK:(idx + 1) * BLOCK]
        kp, vp = kc, vc
    for i in range(4):
        kprev[i] = kp[i]
        vprev[i] = vp[i]
    ya = yatt[...]
    y_ref[:, 0:D_ATTN] = _rms(ya, gn_ref[:, 0:D_ATTN]).astype(y_ref.dtype)

    u = proj_ref[:, O_CC:O_CC + D_CONV] * proj_ref[:, O_CX:O_CX + D_CONV]
    uh = uhalo[...]
    conv = (convw_ref[0:1, :] * u + convw_ref[1:2, :] * _shift_rows(u, uh, 1)
            + convw_ref[2:3, :] * _shift_rows(u, uh, 2))
    uhalo[...] = u[tt - SUBLANES:, :]
    yc = proj_ref[:, O_CB:O_CB + D_CONV] * conv
    y_ref[:, D_ATTN:D_ATTN + D_CONV] = _rms(yc, gn_ref[:, D_ATTN:D_ATTN + D_CONV]).astype(y_ref.dtype)

    lx = proj_ref[:, O_LX:O_LX + D_LRU]
    lh = lxhalo[...]
    xc = lcw_ref[0:1, :] * lx + lcb_ref[...]
    for k in range(1, 4):
        xc = xc + lcw_ref[k:k + 1, :] * _shift_rows(lx, lh, k)
    lxhalo[...] = lx[tt - SUBLANES:, :]
    xcb = xc.astype(BF16)
    r = jax.nn.sigmoid(jnp.dot(xcb, wa_ref[...], preferred_element_type=F32) + ba_ref[...])
    ig = jax.nn.sigmoid(jnp.dot(xcb, wx_ref[...], preferred_element_type=F32) + bx_ref[...])
    log_a = (-LRU_C) * r * jax.nn.softplus(-lam_ref[...])
    a = jnp.exp(log_a)
    th = jnp.tanh(log_a)
    bu = jnp.sqrt(-2.0 * th / (1.0 - th)) * (ig * xc)
    h, h_last = _linear_scan(a, bu, hcarry[0:1, :])
    hcarry[...] = jnp.broadcast_to(h_last, hcarry.shape)
    yl = h * jax.nn.gelu(proj_ref[:, O_LG:O_LG + D_LRU])
    y_ref[:, D_ATTN + D_CONV:] = _rms(yl, gn_ref[:, D_ATTN + D_CONV:]).astype(y_ref.dtype)


def mixer(proj, sinks, conv_w, lru_conv_w, lru_conv_b, wa_bd, ba, wx_bd, bx, lam, mix_norm,
          *, batch, tt=MIXER_TT):
    t = proj.shape[0]
    s = t // batch
    tt = min(tt, s)
    nt = s // tt
    row_map = lambda b, i: (b * nt + i, 0)
    full = lambda shape: pl.BlockSpec(shape, lambda b, i: (0,) * len(shape))
    return pl.pallas_call(
        _mixer_kernel,
        grid=(batch, nt),
        in_specs=[
            pl.BlockSpec(memory_space=pltpu.SMEM),
            pl.BlockSpec((tt, D_IN), row_map),
            full((3, D_CONV)), full((4, D_LRU)), full((1, D_LRU)),
            full((D_LRU, D_LRU)), full((1, D_LRU)), full((D_LRU, D_LRU)), full((1, D_LRU)),
            full((1, D_LRU)), full((1, D_MODEL)),
        ],
        out_specs=pl.BlockSpec((tt, D_MODEL), row_map),
        out_shape=jax.ShapeDtypeStruct((t, D_MODEL), BF16),
        scratch_shapes=[
            pltpu.VMEM((4, BLOCK, LANES), BF16), pltpu.VMEM((4, BLOCK, LANES), BF16),
            pltpu.VMEM((SUBLANES, D_CONV), F32), pltpu.VMEM((SUBLANES, D_LRU), F32),
            pltpu.VMEM((SUBLANES, D_LRU), F32), pltpu.VMEM((tt, D_ATTN), F32),
        ],
        compiler_params=_params(("arbitrary", "arbitrary"), MIXER_VMEM),
        name="mixer",
    )(sinks, proj, conv_w, lru_conv_w, lru_conv_b.reshape(1, -1), wa_bd, ba.reshape(1, -1),
      wx_bd, bx.reshape(1, -1), lam.reshape(1, -1), mix_norm.reshape(1, -1))


def _route(hn, rw_ref, carry):
    tm = hn.shape[0]
    h_hi = hn.astype(BF16)
    h_lo = (hn - h_hi.astype(F32)).astype(BF16)
    res = jnp.dot(jnp.concatenate([h_hi, h_lo], axis=1), rw_ref[...], preferred_element_type=F32)
    logits = res[:, :LANES] + res[:, LANES:]
    lane = lax.broadcasted_iota(jnp.int32, (tm, LANES), 1).astype(F32)
    neg = -jnp.inf
    l1 = jnp.where(lane < N_EXPERTS, logits, neg)
    m1 = jnp.max(l1, axis=-1, keepdims=True)
    i1 = jnp.min(jnp.where(l1 == m1, lane, float(LANES)), axis=-1, keepdims=True)
    l2 = jnp.where(lane == i1, neg, l1)
    m2 = jnp.max(l2, axis=-1, keepdims=True)
    i2 = jnp.min(jnp.where(l2 == m2, lane, float(LANES)), axis=-1, keepdims=True)
    e21 = jnp.exp(m2 - m1)
    g1 = 1.0 / (1.0 + e21)
    g2 = e21 / (1.0 + e21)
    sel1 = lane == i1
    sel2 = lane == i2
    onehot = jnp.where(sel1 | sel2, 1.0, 0.0).astype(BF16)
    rr = lax.broadcasted_iota(jnp.int32, (tm, tm), 0)
    cc = lax.broadcasted_iota(jnp.int32, (tm, tm), 1)
    tri = jnp.where(rr >= cc, 1.0, 0.0).astype(BF16)
    counts = jnp.dot(tri, onehot, preferred_element_type=F32) + carry
    r1 = jnp.sum(jnp.where(sel1, counts, 0.0), axis=-1, keepdims=True) - 1.0
    r2 = jnp.sum(jnp.where(sel2, counts, 0.0), axis=-1, keepdims=True) - 1.0
    packed = jnp.where(lane == 0.0, i1, jnp.where(lane == 1.0, i2, jnp.where(lane == 2.0, r1,
             jnp.where(lane == 3.0, r2, jnp.where(lane == 4.0, g1, jnp.where(lane == 5.0, g2, 0.0))))))
    return packed, counts[tm - 1:tm, :]


def _outproj_kernel(y_ref, w_ref, x_ref, xo_ref):
    xo_ref[...] = x_ref[...] + jnp.dot(y_ref[...], w_ref[...], preferred_element_type=F32)


def _outproj_route_kernel(y_ref, w_ref, x_ref, g_ref, rw_ref, xo_ref, ho_ref, route_ref, cnt_ref,
                          carry, hprev):
    i = pl.program_id(0)

    @pl.when(i == 0)
    def _():
        carry[...] = jnp.zeros_like(carry)
        hprev[...] = jnp.zeros_like(hprev)

    packed, last = _route(hprev[...], rw_ref, carry[0:1, :])
    route_ref[...] = packed
    counts = jnp.where(i > 0, jnp.broadcast_to(last, carry.shape), carry[...])
    carry[...] = counts
    cnt_ref[...] = counts
    xn = x_ref[...] + jnp.dot(y_ref[...], w_ref[...], preferred_element_type=F32)
    xo_ref[...] = xn
    hn = _rms(xn, g_ref[...])
    ho_ref[...] = hn
    hprev[...] = hn


def outproj(y, w, x, *, tm=OUTPROJ_TM):
    t, d = x.shape
    tm = min(tm, t)
    rowb = pl.BlockSpec((tm, d), lambda i: (i, 0))
    return pl.pallas_call(
        _outproj_kernel,
        grid=(t // tm,),
        in_specs=[rowb, pl.BlockSpec((d, d), lambda i: (0, 0)), rowb],
        out_specs=rowb,
        out_shape=jax.ShapeDtypeStruct((t, d), F32),
        compiler_params=_params(("arbitrary",), OUTPROJ_VMEM),
        name="outproj",
    )(y, w, x)


def outproj_route(y, w, x, g, rw2, *, tm=OUTPROJ_TM):
    t, d = x.shape
    tm = min(tm, t)
    n = t // tm
    rowb = pl.BlockSpec((tm, d), lambda i: (jnp.minimum(i, n - 1), 0))
    return pl.pallas_call(
        _outproj_route_kernel,
        grid=(n + 1,),
        in_specs=[rowb, pl.BlockSpec((d, d), lambda i: (0, 0)), rowb, pl.BlockSpec((1, d), lambda i: (0, 0)),
                  pl.BlockSpec((2 * d, 2 * LANES), lambda i: (0, 0))],
        out_specs=[rowb, rowb, pl.BlockSpec((tm, LANES), lambda i: (jnp.maximum(i - 1, 0), 0)),
                   pl.BlockSpec((SUBLANES, LANES), lambda i: (0, 0))],
        out_shape=[jax.ShapeDtypeStruct((t, d), F32), jax.ShapeDtypeStruct((t, d), F32),
                   jax.ShapeDtypeStruct((t, LANES), F32), jax.ShapeDtypeStruct((SUBLANES, LANES), F32)],
        scratch_shapes=[pltpu.VMEM((SUBLANES, LANES), F32), pltpu.VMEM((tm, d), F32)],
        compiler_params=_params(("arbitrary",), ROUTE_VMEM),
        name="outproj_route",
    )(y, w, x, g.reshape(1, d), rw2)


def _swiglu_step(xb, wg, wu, wd):
    g = jnp.dot(xb, wg, preferred_element_type=F32)
    u = jnp.dot(xb, wu, preferred_element_type=F32)
    a = (g * jax.nn.sigmoid(g) * u).astype(BF16)
    return jnp.dot(a, wd, preferred_element_type=F32)


W_SLOTS = 3


def _weight_copies(wg_hbm, wu_hbm, wd_hbm, f, tf, wgr, wur, wdr, sem, slot, expert=None):
    cols = pl.ds(pl.multiple_of(f * tf, tf), tf)
    if expert is not None:
        wg_hbm, wu_hbm, wd_hbm = wg_hbm.at[expert], wu_hbm.at[expert], wd_hbm.at[expert]
    return (pltpu.make_async_copy(wg_hbm.at[:, cols], wgr.at[slot], sem.at[0, slot]),
            pltpu.make_async_copy(wu_hbm.at[:, cols], wur.at[slot], sem.at[1, slot]),
            pltpu.make_async_copy(wd_hbm.at[cols, :], wdr.at[slot], sem.at[2, slot]))


def _ffn_kernel(x_ref, g_ref, wg_hbm, wu_hbm, wd_hbm, xo_ref, hb_ref, wgr, wur, wdr, sem):
    i = pl.program_id(0)
    f = pl.program_id(1)
    nf = pl.num_programs(1)
    tf = wgr.shape[2]
    step = i * nf + f
    last = pl.num_programs(0) * nf - 1

    def fetch(s):
        for cp in _weight_copies(wg_hbm, wu_hbm, wd_hbm, lax.rem(s, nf), tf, wgr, wur, wdr, sem,
                                 lax.rem(s, W_SLOTS)):
            cp.start()

    @pl.when(step == 0)
    def _():
        fetch(step)
        fetch(step + 1)

    @pl.when(step + 2 <= last)
    def _():
        fetch(step + 2)

    @pl.when(f == 0)
    def _():
        x = x_ref[...]
        hb_ref[...] = _rms(x, g_ref[...]).astype(BF16)
        xo_ref[...] = x

    slot = lax.rem(step, W_SLOTS)
    for cp in _weight_copies(wg_hbm, wu_hbm, wd_hbm, f, tf, wgr, wur, wdr, sem, slot):
        cp.wait()
    xo_ref[...] += _swiglu_step(hb_ref[...], wgr[slot].astype(BF16), wur[slot].astype(BF16),
                                wdr[slot].astype(BF16))


def ffn_dense(x, g, wg, wu, wd, *, tm=FFN_TM, tf=FFN_TF):
    t, d = x.shape
    ff = wg.shape[1]
    tm = min(tm, t)
    rowb = pl.BlockSpec((tm, d), lambda i, f: (i, 0))
    return pl.pallas_call(
        _ffn_kernel,
        grid=(t // tm, ff // tf),
        in_specs=[rowb, pl.BlockSpec((1, d), lambda i, f: (0, 0)),
                  pl.BlockSpec(memory_space=pl.ANY), pl.BlockSpec(memory_space=pl.ANY),
                  pl.BlockSpec(memory_space=pl.ANY)],
        out_specs=rowb,
        out_shape=jax.ShapeDtypeStruct((t, d), F32),
        scratch_shapes=[pltpu.VMEM((tm, d), BF16), pltpu.VMEM((W_SLOTS, d, tf), F32),
                        pltpu.VMEM((W_SLOTS, d, tf), F32), pltpu.VMEM((W_SLOTS, tf, d), F32),
                        pltpu.SemaphoreType.DMA((3, W_SLOTS))],
        compiler_params=_params(("arbitrary", "arbitrary"), FFN_VMEM),
        name="ffn_dense",
    )(x, g.reshape(1, d), wg, wu, wd)


def _row_copy(src, src_row, dst, dst_row, sem):
    return pltpu.make_async_copy(src.at[pl.ds(src_row, 1)], dst.at[pl.ds(dst_row, 1)], sem)


def _for_rows(n, fn):
    def body(r, c):
        fn(r)
        return c

    lax.fori_loop(0, n, body, 0)


M_ALWAYS = 512
M_CHUNK = 256


def _expert_kernel(te_ref, tb_ref, nv_ref, xs_ref, wg_hbm, wu_hbm, wd_hbm, ys_ref, xb, wgr, wur, wdr, sem):
    i = pl.program_id(0)
    f = pl.program_id(1)
    nt = pl.num_programs(0)
    nf = pl.num_programs(1)
    tm = xb.shape[0]
    tf = wgr.shape[2]
    nv = nv_ref[i]
    step = i * nf + f

    def copies(tile, fs, s):
        return _weight_copies(wg_hbm, wu_hbm, wd_hbm, fs, tf, wgr, wur, wdr, sem, lax.rem(s, W_SLOTS),
                              expert=te_ref[tile])

    def fetch_ahead(k):
        over = f + k >= nf
        tile = jnp.minimum(i + over.astype(jnp.int32), nt - 1)
        fs = f + k - nf * over.astype(jnp.int32)

        @pl.when(jnp.logical_not(over) | ((i + 1 < nt) & (nv_ref[tile] > 0)))
        def _():
            for cp in copies(tile, fs, step + k):
                cp.start()

    @pl.when(nv > 0)
    def _():
        @pl.when(step == 0)
        def _():
            fetch_ahead(0)
            fetch_ahead(1)

        fetch_ahead(2)

        @pl.when(f == 0)
        def _():
            xb[...] = xs_ref[...].astype(BF16)
            ys_ref[...] = jnp.zeros_like(ys_ref)

        slot = lax.rem(step, W_SLOTS)
        for cp in copies(i, f, step):
            cp.wait()
        m0 = min(M_ALWAYS, tm)

        @pl.when(nv > tm - M_CHUNK)
        def _():
            ys_ref[...] += _swiglu_step(xb[...], wgr[slot].astype(BF16), wur[slot].astype(BF16),
                                        wdr[slot].astype(BF16))

        @pl.when(nv <= tm - M_CHUNK)
        def _():
            ys_ref[0:m0, :] += _swiglu_step(xb[0:m0, :], wgr[slot].astype(BF16), wur[slot].astype(BF16),
                                            wdr[slot].astype(BF16))
            for c in range(m0, tm - M_CHUNK, M_CHUNK):
                @pl.when(c < nv)
                def _():
                    rows = pl.ds(c, M_CHUNK)
                    ys_ref[rows, :] += _swiglu_step(xb[rows, :], wgr[slot].astype(BF16),
                                                    wur[slot].astype(BF16), wdr[slot].astype(BF16))

    @pl.when((nv == 0) & (f == nf - 1))
    def _():
        ys_ref[...] = jnp.zeros_like(ys_ref)


def moe_experts(xs, wg, wu, wd, tile_expert, tile_block, tile_rows, *, tm, tf=MOE_TF):
    slots, d = xs.shape
    fe = wg.shape[2]
    nf = fe // tf
    nt = slots // tm

    grid_spec = pltpu.PrefetchScalarGridSpec(
        num_scalar_prefetch=3,
        grid=(nt, nf),
        in_specs=[
            pl.BlockSpec((tm, d), lambda i, f, te, tb, nv: (tb[i], 0)),
            pl.BlockSpec(memory_space=pl.ANY), pl.BlockSpec(memory_space=pl.ANY),
            pl.BlockSpec(memory_space=pl.ANY),
        ],
        out_specs=pl.BlockSpec((tm, d), lambda i, f, te, tb, nv: (i, 0)),
        scratch_shapes=[pltpu.VMEM((tm, d), BF16), pltpu.VMEM((W_SLOTS, d, tf), F32),
                        pltpu.VMEM((W_SLOTS, d, tf), F32), pltpu.VMEM((W_SLOTS, tf, d), F32),
                        pltpu.SemaphoreType.DMA((3, W_SLOTS))],
    )
    return pl.pallas_call(
        _expert_kernel,
        grid_spec=grid_spec,
        out_shape=jax.ShapeDtypeStruct((slots, d), F32),
        compiler_params=_params(("arbitrary", "arbitrary"), MOE_VMEM),
        name="moe_experts",
    )(tile_expert, tile_block, tile_rows, xs, wg, wu, wd)


def _dispatch_kernel(pos_ref, trows_ref, h_ref, xs_hbm, zbuf, sem, zsem):
    tmd = h_ref.shape[0]
    tm = zbuf.shape[0]

    @pl.when(pl.program_id(0) == 0)
    def _():
        zbuf[...] = jnp.zeros_like(zbuf)

        def tile_fill(j):
            return pltpu.make_async_copy(zbuf, xs_hbm.at[pl.ds(j * tm, tm)], zsem)

        def fill(j):
            @pl.when(trows_ref[j] < tm)
            def _():
                tile_fill(j).start()

        def fill_wait(j):
            @pl.when(trows_ref[j] < tm)
            def _():
                tile_fill(j).wait()

        _for_rows(trows_ref.shape[0], fill)
        _for_rows(trows_ref.shape[0], fill_wait)

    for tok in range(tmd):
        for k in range(2):
            _row_copy(h_ref, tok, xs_hbm, pos_ref[2 * tok + k], sem).start(priority=k)
    for _ in range(2):
        pltpu.make_async_copy(h_ref, xs_hbm.at[pl.ds(0, tmd)], sem).wait()


def moe_dispatch(hn, pos, tile_rows, *, tm, tmd=DISPATCH_TM):
    t, d = hn.shape
    tmd = min(tmd, t)
    return pl.pallas_call(
        _dispatch_kernel,
        grid=(t // tmd,),
        in_specs=[pl.BlockSpec((2 * tmd,), lambda i: (i,), memory_space=pltpu.SMEM),
                  pl.BlockSpec(memory_space=pltpu.SMEM),
                  pl.BlockSpec((tmd, d), lambda i: (i, 0))],
        out_specs=pl.BlockSpec(memory_space=pl.ANY),
        out_shape=jax.ShapeDtypeStruct((tile_rows.shape[0] * tm, d), hn.dtype),
        scratch_shapes=[pltpu.VMEM((tm, d), hn.dtype), pltpu.SemaphoreType.DMA(()),
                        pltpu.SemaphoreType.DMA(())],
        compiler_params=_params(("arbitrary",), DISPATCH_VMEM),
        name="moe_dispatch",
    )(pos, tile_rows, hn)


def _combine_kernel(pos_ref, posn_ref, route_ref, x_ref, g_ref, ys_hbm, o_ref, buf, sem):
    i = pl.program_id(0)
    n = pl.num_programs(0)
    tmc = x_ref.shape[0]

    def gather(pref, slot):
        for tok in range(tmc):
            for k in range(2):
                _row_copy(ys_hbm, pref[2 * tok + k], buf.at[slot, k], tok, sem.at[slot]).start(priority=k)

    @pl.when(i == 0)
    def _():
        gather(pos_ref, 0)

    @pl.when(i + 1 < n)
    def _():
        gather(posn_ref, (i + 1) % 2)

    cur = i % 2
    for k in range(2):
        pltpu.make_async_copy(ys_hbm.at[pl.ds(0, tmc)], buf.at[cur, k], sem.at[cur]).wait()
    xn = x_ref[...] + (route_ref[:, 4:5] * buf[cur, 0] + route_ref[:, 5:6] * buf[cur, 1])
    o_ref[...] = _rms(xn, g_ref[...]).astype(o_ref.dtype)


def moe_combine(ys, pos, route, x, g, *, tmc=COMBINE_TM):
    t, d = x.shape
    tmc = min(tmc, t)
    n = t // tmc
    rowb = pl.BlockSpec((tmc, d), lambda i: (i, 0))
    cur = pl.BlockSpec((2 * tmc,), lambda i: (i,), memory_space=pltpu.SMEM)
    nxt = pl.BlockSpec((2 * tmc,), lambda i: (jnp.minimum(i + 1, n - 1),), memory_space=pltpu.SMEM)
    return pl.pallas_call(
        _combine_kernel,
        grid=(n,),
        in_specs=[cur, nxt,
                  pl.BlockSpec((tmc, LANES), lambda i: (i, 0)), rowb, pl.BlockSpec((1, d), lambda i: (0, 0)),
                  pl.BlockSpec(memory_space=pl.ANY)],
        out_specs=rowb,
        out_shape=jax.ShapeDtypeStruct((t, d), F32),
        scratch_shapes=[pltpu.VMEM((2, 2, tmc, d), F32), pltpu.SemaphoreType.DMA((2,))],
        compiler_params=_params(("arbitrary",), COMBINE_VMEM),
        name="moe_combine",
    )(pos, pos, route, x, g.reshape(1, d), ys)


def _tile_schedule(counts, tm, nt):
    cnt = counts.astype(jnp.int32)
    ntile = (cnt + tm - 1) // tm
    tend = jnp.cumsum(ntile)
    tstart = tend - ntile
    total = tend[-1]
    tid = jnp.arange(nt, dtype=jnp.int32)
    active = tid < total
    tclamp = jnp.minimum(tid, total - 1)
    te = jnp.sum((tclamp[:, None] >= tend[None, :]).astype(jnp.int32), axis=1)
    rows = jnp.clip(cnt[te] - (tclamp - tstart[te]) * tm, 0, tm)
    return tstart * tm, te, tclamp, jnp.where(active, rows, 0)


def moe_block(hn, route, counts, x, g, wg, wu, wd, *, tm=MOE_TM):
    t, d = hn.shape
    nt = -(-2 * t // tm) + N_EXPERTS
    off, te, tblk, trows = _tile_schedule(counts[0, :N_EXPERTS], tm, nt)
    pos = (off[route[:, 0:2].astype(jnp.int32)] + route[:, 2:4].astype(jnp.int32)).reshape(-1)
    xs = moe_dispatch(hn, pos, trows, tm=tm)
    ys = moe_experts(xs, wg, wu, wd, te, tblk, trows, tm=tm)
    return moe_combine(ys, pos, route, x, g)


def _block_diag(w):
    h, n, _ = w.shape
    eye = jnp.eye(h, dtype=w.dtype)
    return (eye[:, None, :, None] * w[:, :, None, :]).reshape(h * n, h * n)


def kernel(x, attn_norm, w_in, attn_sinks, conv_w, lru_conv_w, lru_conv_b, lru_wa, lru_ba, lru_wx,
           lru_bx, lru_lambda, mix_norm, w_out, ffn_norm, dense_w_gate, dense_w_up, dense_w_down,
           router_w, expert_w_gate, expert_w_up, expert_w_down, final_norm):
    b, s, d = x.shape
    depth = w_in.shape[0]
    assert depth == 2 and d == D_MODEL
    xf = x.reshape(b * s, d)
    out = None
    for layer in range(depth):
        proj = inproj(xf, attn_norm[layer], w_in[layer].astype(BF16))
        y = mixer(proj, attn_sinks[layer], conv_w[layer], lru_conv_w[layer], lru_conv_b[layer],
                  _block_diag(lru_wa[layer]).astype(BF16), lru_ba[layer],
                  _block_diag(lru_wx[layer]).astype(BF16), lru_bx[layer], lru_lambda[layer],
                  mix_norm[layer], batch=b)
        j = layer // 2
        if layer % 2 == 0:
            xf = outproj(y, w_out[layer].astype(BF16), xf)
            xf = ffn_dense(xf, ffn_norm[layer], dense_w_gate[j], dense_w_up[j], dense_w_down[j])
        else:
            rw = jnp.pad(router_w[j], ((0, 0), (0, LANES - N_EXPERTS)))
            rw_hi = rw.astype(BF16)
            rw_lo = (rw - rw_hi.astype(F32)).astype(BF16)
            rw2 = jnp.concatenate([jnp.concatenate([rw_hi, rw_lo], axis=1),
                                   jnp.concatenate([rw_hi, jnp.zeros_like(rw_lo)], axis=1)], axis=0)
            xf, hn, route, counts = outproj_route(y, w_out[layer].astype(BF16), xf, ffn_norm[layer], rw2)
            out = moe_block(hn, route, counts, xf, final_norm, expert_w_gate[j], expert_w_up[j],
                            expert_w_down[j])
    return out.reshape(b, s, d)
```

```python
import jax
import jax.numpy as jnp
from jax import lax
from jax.experimental import pallas as pl
from jax.experimental.pallas import tpu as pltpu

F32 = jnp.float32
BF16 = jnp.bfloat16

D_MODEL = 2048
HEAD_DIM = 64
D_ATTN = 1024
N_HEADS = 16
N_KV_HEADS = 2
GROUP = N_HEADS // N_KV_HEADS
BLOCK = 128
D_KV = N_KV_HEADS * HEAD_DIM
D_CONV = 512
D_LRU = 512
LRU_C = 8.0
LOG2E = 1.4426950408889634
D_IN = 3840
N_EXPERTS = 8
EPS = 1e-6
LANES = 128
SUBLANES = 8

O_Q, O_K, O_V = 0, 1024, 1152
O_CB, O_CC, O_CX = 1280, 1792, 2304
O_LX, O_LG = 2816, 3328

MIB = 2 ** 20

INPROJ_TM, INPROJ_TN, INPROJ_VMEM = 1024, 1280, 56
MIXER_TT, MIXER_VMEM = 512, 48
OUTPROJ_TM, OUTPROJ_VMEM, ROUTE_VMEM, ROUTER_TM = 512, 48, 56, 1024
FFN_TM, FFN_TF, FFN_VMEM = 1024, 256, 60
MOE_TM, MOE_TF, MOE_VMEM = 1024, 256, 60
DISPATCH_TM, DISPATCH_VMEM = 1024, 40
COMBINE_TM, COMBINE_VMEM = 256, 48


def _params(sem, vmem_mib):
    return pltpu.CompilerParams(dimension_semantics=sem, vmem_limit_bytes=vmem_mib * MIB)


def _rms(x, g):
    return x * lax.rsqrt(jnp.mean(x * x, axis=-1, keepdims=True) + EPS) * g


def _inproj_kernel(x_ref, g_ref, w_ref, o_ref, hb_ref):
    @pl.when(pl.program_id(1) == 0)
    def _():
        hb_ref[...] = _rms(x_ref[...], g_ref[...]).astype(BF16)

    o_ref[...] = jnp.dot(hb_ref[...], w_ref[...], preferred_element_type=F32)


def inproj(x, g, w, *, tm=INPROJ_TM, tn=INPROJ_TN):
    t, d = x.shape
    n = w.shape[1]
    tm = min(tm, t)
    return pl.pallas_call(
        _inproj_kernel,
        grid=(t // tm, n // tn),
        in_specs=[pl.BlockSpec((tm, d), lambda i, j: (i, 0)), pl.BlockSpec((1, d), lambda i, j: (0, 0)),
                  pl.BlockSpec((d, tn), lambda i, j: (0, j))],
        out_specs=pl.BlockSpec((tm, tn), lambda i, j: (i, j)),
        out_shape=jax.ShapeDtypeStruct((t, n), F32),
        scratch_shapes=[pltpu.VMEM((tm, d), BF16)],
        compiler_params=_params(("arbitrary", "arbitrary"), INPROJ_VMEM),
        name="inproj",
    )(x, g.reshape(1, d), w)


def _shift_rows(u, halo, k):
    r = pltpu.roll(u, k, axis=0)
    hr = pltpu.roll(halo, k, axis=0)
    row = lax.broadcasted_iota(jnp.int32, hr.shape, 0)
    top = jnp.where(row < k, hr, r[:SUBLANES])
    return jnp.concatenate([top, r[SUBLANES:]], axis=0)


def _linear_scan(a, b, h0):
    n, c = a.shape
    g = n // SUBLANES
    a = a.reshape(g, SUBLANES, c)
    b = b.reshape(g, SUBLANES, c)
    sub = lax.broadcasted_iota(jnp.int32, a.shape, 1)
    s = 1
    while s < SUBLANES:
        a_sh = pltpu.roll(a, s, axis=1)
        b_sh = pltpu.roll(b, s, axis=1)
        valid = sub >= s
        b = jnp.where(valid, a * b_sh, 0.0) + b
        a = jnp.where(valid, a * a_sh, a)
        s *= 2
    hs = []
    carry = h0
    for i in range(g):
        h = a[i] * carry + b[i]
        carry = h[SUBLANES - 1:SUBLANES, :]
        hs.append(h)
    return jnp.concatenate(hs, axis=0), carry


def _half_variants(x):
    lane = lax.broadcasted_iota(jnp.int32, x.shape, 1)
    lo = lane < HEAD_DIM
    sw = pltpu.roll(x, HEAD_DIM, axis=1)
    z = jnp.zeros_like(x)
    return [jnp.where(lo, x, z).astype(BF16), jnp.where(lo, z, sw).astype(BF16),
            jnp.where(lo, sw, z).astype(BF16), jnp.where(lo, z, x).astype(BF16)]


def _mixer_kernel(sinks_ref, proj_ref, convw_ref, lcw_ref, lcb_ref, wa_ref, ba_ref, wx_ref, bx_ref,
                  lam_ref, gn_ref, y_ref, kprev, vprev, uhalo, lxhalo, hcarry, yatt):
    tb = pl.program_id(1)
    tt = proj_ref.shape[0]

    @pl.when(tb == 0)
    def _():
        kprev[...] = jnp.zeros_like(kprev)
        vprev[...] = jnp.zeros_like(vprev)
        uhalo[...] = jnp.zeros_like(uhalo)
        lxhalo[...] = jnp.zeros_like(lxhalo)
        hcarry[...] = jnp.zeros_like(hcarry)

    rows4 = GROUP // 2 * BLOCK
    row = lax.broadcasted_iota(jnp.int32, (rows4, LANES), 0)
    col = lax.broadcasted_iota(jnp.int32, (rows4, LANES), 1)
    in_cur = col <= (row & (BLOCK - 1))
    even = col < HEAD_DIM
    even1 = lax.broadcasted_iota(jnp.int32, (BLOCK, LANES), 1) < HEAD_DIM
    ones_even = jnp.where(even1, 1.0, 0.0).astype(BF16)
    ones_odd = jnp.where(even1, 0.0, 1.0).astype(BF16)
    den_cols = jnp.concatenate([ones_even, ones_even, ones_odd, ones_odd], axis=0)
    first_bias = jnp.where(tb > 0, 0.0, -jnp.inf).astype(F32)
    for sb in range(tt // BLOCK):
        rows = slice(sb * BLOCK, (sb + 1) * BLOCK)
        kc = _half_variants(proj_ref[rows, O_K:O_K + D_KV])
        vc = _half_variants(proj_ref[rows, O_V:O_V + D_KV])
        if sb == 0:
            kp = [kprev[i] for i in range(4)]
            vp = [vprev[i] for i in range(4)]
        for kvh in range(N_KV_HEADS):
            pairs = range(kvh * GROUP // 2, (kvh + 1) * GROUP // 2)
            lo, hi = 2 * kvh, 2 * kvh + 1
            q4 = jnp.concatenate([proj_ref[rows, p * LANES:(p + 1) * LANES] for p in pairs], axis=0)
            q4 = (q4 * (HEAD_DIM ** -0.5 * LOG2E)).astype(BF16)
            kk = jnp.concatenate([kc[lo], kp[lo], kc[hi], kp[hi]], axis=0)
            s = lax.dot_general(q4, kk, (((1,), (1,)), ((), ())), preferred_element_type=F32)
            pps, sink_terms = [], []
            for par in range(2):
                c0 = 2 * BLOCK * par
                s_prev = s[:, c0 + BLOCK:c0 + 2 * BLOCK]
                if sb == 0:
                    s_prev = s_prev + first_bias
                ssel = jnp.where(in_cur, s[:, c0:c0 + BLOCK], s_prev)
                sk = [sinks_ref[2 * p + par] * LOG2E for p in pairs]
                sink = jnp.where(row < BLOCK, sk[0], jnp.where(row < 2 * BLOCK, sk[1],
                                 jnp.where(row < 3 * BLOCK, sk[2], sk[3])))
                m = jnp.maximum(jnp.max(ssel, axis=-1, keepdims=True), sink)
                p2 = jnp.exp2(ssel - m)
                pps += [jnp.where(in_cur, p2, 0.0).astype(BF16), jnp.where(in_cur, 0.0, p2).astype(BF16)]
                sink_terms.append(jnp.exp2(sink - m))
            vv = jnp.concatenate([vc[lo], vp[lo], vc[hi], vp[hi]], axis=0)
            r = jnp.dot(jnp.concatenate(pps, axis=1), jnp.concatenate([vv, den_cols], axis=1),
                        preferred_element_type=F32)
            den = r[:, LANES:] + jnp.where(even, sink_terms[0], sink_terms[1])
            o4 = r[:, :LANES] / den
            for idx, p in enumerate(pairs):
                yatt[rows, p * LANES:(p + 1) * LANES] = o4[idx * BLOCK:(idx + 1) * BLOCK]
        kp, vp = kc, vc
    for i in range(4):
        kprev[i] = kp[i]
        vprev[i] = vp[i]
    ya = yatt[...]
    y_ref[:, 0:D_ATTN] = _rms(ya, gn_ref[:, 0:D_ATTN]).astype(y_ref.dtype)

    u = proj_ref[:, O_CC:O_CC + D_CONV] * proj_ref[:, O_CX:O_CX + D_CONV]
    uh = uhalo[...]
    conv = (convw_ref[0:1, :] * u + convw_ref[1:2, :] * _shift_rows(u, uh, 1)
            + convw_ref[2:3, :] * _shift_rows(u, uh, 2))
    uhalo[...] = u[tt - SUBLANES:, :]
    yc = proj_ref[:, O_CB:O_CB + D_CONV] * conv
    y_ref[:, D_ATTN:D_ATTN + D_CONV] = _rms(yc, gn_ref[:, D_ATTN:D_ATTN + D_CONV]).astype(y_ref.dtype)

    lx = proj_ref[:, O_LX:O_LX + D_LRU]
    lh = lxhalo[...]
    xc = lcw_ref[0:1, :] * lx + lcb_ref[...]
    for k in range(1, 4):
        xc = xc + lcw_ref[k:k + 1, :] * _shift_rows(lx, lh, k)
    lxhalo[...] = lx[tt - SUBLANES:, :]
    xcb = xc.astype(BF16)
    r = jax.nn.sigmoid(jnp.dot(xcb, wa_ref[...], preferred_element_type=F32) + ba_ref[...])
    ig = jax.nn.sigmoid(jnp.dot(xcb, wx_ref[...], preferred_element_type=F32) + bx_ref[...])
    log_a = (-LRU_C) * r * jax.nn.softplus(-lam_ref[...])
    a = jnp.exp(log_a)
    th = jnp.tanh(log_a)
    bu = jnp.sqrt(-2.0 * th / (1.0 - th)) * (ig * xc)
    h, h_last = _linear_scan(a, bu, hcarry[0:1, :])
    hcarry[...] = jnp.broadcast_to(h_last, hcarry.shape)
    yl = h * jax.nn.gelu(proj_ref[:, O_LG:O_LG + D_LRU])
    y_ref[:, D_ATTN + D_CONV:] = _rms(yl, gn_ref[:, D_ATTN + D_CONV:]).astype(y_ref.dtype)


def mixer(proj, sinks, conv_w, lru_conv_w, lru_conv_b, wa_bd, ba, wx_bd, bx, lam, mix_norm,
          *, batch, tt=MIXER_TT):
    t = proj.shape[0]
    s = t // batch
    tt = min(tt, s)
    nt = s // tt
    row_map = lambda b, i: (b * nt + i, 0)
    full = lambda shape: pl.BlockSpec(shape, lambda b, i: (0,) * len(shape))
    return pl.pallas_call(
        _mixer_kernel,
        grid=(batch, nt),
        in_specs=[
            pl.BlockSpec(memory_space=pltpu.SMEM),
            pl.BlockSpec((tt, D_IN), row_map),
            full((3, D_CONV)), full((4, D_LRU)), full((1, D_LRU)),
            full((D_LRU, D_LRU)), full((1, D_LRU)), full((D_LRU, D_LRU)), full((1, D_LRU)),
            full((1, D_LRU)), full((1, D_MODEL)),
        ],
        out_specs=pl.BlockSpec((tt, D_MODEL), row_map),
        out_shape=jax.ShapeDtypeStruct((t, D_MODEL), BF16),
        scratch_shapes=[
            pltpu.VMEM((4, BLOCK, LANES), BF16), pltpu.VMEM((4, BLOCK, LANES), BF16),
            pltpu.VMEM((SUBLANES, D_CONV), F32), pltpu.VMEM((SUBLANES, D_LRU), F32),
            pltpu.VMEM((SUBLANES, D_LRU), F32), pltpu.VMEM((tt, D_ATTN), F32),
        ],
        compiler_params=_params(("arbitrary", "arbitrary"), MIXER_VMEM),
        name="mixer",
    )(sinks, proj, conv_w, lru_conv_w, lru_conv_b.reshape(1, -1), wa_bd, ba.reshape(1, -1),
      wx_bd, bx.reshape(1, -1), lam.reshape(1, -1), mix_norm.reshape(1, -1))


def _route(hn, rw_ref, carry):
    tm = hn.shape[0]
    h_hi = hn.astype(BF16)
    h_lo = (hn - h_hi.astype(F32)).astype(BF16)
    res = jnp.dot(jnp.concatenate([h_hi, h_lo], axis=1), rw_ref[...], preferred_element_type=F32)
    logits = res[:, :LANES] + res[:, LANES:]
    lane = lax.broadcasted_iota(jnp.int32, (tm, LANES), 1).astype(F32)
    neg = -jnp.inf
    l1 = jnp.where(lane < N_EXPERTS, logits, neg)
    m1 = jnp.max(l1, axis=-1, keepdims=True)
    i1 = jnp.min(jnp.where(l1 == m1, lane, float(LANES)), axis=-1, keepdims=True)
    l2 = jnp.where(lane == i1, neg, l1)
    m2 = jnp.max(l2, axis=-1, keepdims=True)
    i2 = jnp.min(jnp.where(l2 == m2, lane, float(LANES)), axis=-1, keepdims=True)
    e21 = jnp.exp(m2 - m1)
    g1 = 1.0 / (1.0 + e21)
    g2 = e21 / (1.0 + e21)
    sel1 = lane == i1
    sel2 = lane == i2
    onehot = jnp.where(sel1 | sel2, 1.0, 0.0).astype(BF16)
    rr = lax.broadcasted_iota(jnp.int32, (tm, tm), 0)
    cc = lax.broadcasted_iota(jnp.int32, (tm, tm), 1)
    tri = jnp.where(rr >= cc, 1.0, 0.0).astype(BF16)
    counts = jnp.dot(tri, onehot, preferred_element_type=F32) + carry
    r1 = jnp.sum(jnp.where(sel1, counts, 0.0), axis=-1, keepdims=True) - 1.0
    r2 = jnp.sum(jnp.where(sel2, counts, 0.0), axis=-1, keepdims=True) - 1.0
    packed = jnp.where(lane == 0.0, i1, jnp.where(lane == 1.0, i2, jnp.where(lane == 2.0, r1,
             jnp.where(lane == 3.0, r2, jnp.where(lane == 4.0, g1, jnp.where(lane == 5.0, g2, 0.0))))))
    return packed, counts[tm - 1:tm, :]


def _outproj_kernel(y_ref, w_ref, x_ref, xo_ref):
    xo_ref[...] = x_ref[...] + jnp.dot(y_ref[...], w_ref[...], preferred_element_type=F32)


def _outproj_norm_kernel(y_ref, w_ref, x_ref, g_ref, xo_ref, ho_ref):
    xn = x_ref[...] + jnp.dot(y_ref[...], w_ref[...], preferred_element_type=F32)
    xo_ref[...] = xn
    ho_ref[...] = _rms(xn, g_ref[...])


def _router_kernel(h_ref, rw_ref, route_ref, cnt_ref, carry):
    @pl.when(pl.program_id(0) == 0)
    def _():
        carry[...] = jnp.zeros_like(carry)

    packed, last = _route(h_ref[...], rw_ref, carry[0:1, :])
    route_ref[...] = packed
    carry[...] = jnp.broadcast_to(last, carry.shape)
    cnt_ref[...] = jnp.broadcast_to(last, cnt_ref.shape)


def outproj(y, w, x, *, tm=OUTPROJ_TM):
    t, d = x.shape
    tm = min(tm, t)
    rowb = pl.BlockSpec((tm, d), lambda i: (i, 0))
    return pl.pallas_call(
        _outproj_kernel,
        grid=(t // tm,),
        in_specs=[rowb, pl.BlockSpec((d, d), lambda i: (0, 0)), rowb],
        out_specs=rowb,
        out_shape=jax.ShapeDtypeStruct((t, d), F32),
        compiler_params=_params(("arbitrary",), OUTPROJ_VMEM),
        name="outproj",
    )(y, w, x)


def outproj_norm(y, w, x, g, *, tm=OUTPROJ_TM):
    t, d = x.shape
    tm = min(tm, t)
    rowb = pl.BlockSpec((tm, d), lambda i: (i, 0))
    return pl.pallas_call(
        _outproj_norm_kernel,
        grid=(t // tm,),
        in_specs=[rowb, pl.BlockSpec((d, d), lambda i: (0, 0)), rowb, pl.BlockSpec((1, d), lambda i: (0, 0))],
        out_specs=[rowb, rowb],
        out_shape=[jax.ShapeDtypeStruct((t, d), F32), jax.ShapeDtypeStruct((t, d), F32)],
        compiler_params=_params(("arbitrary",), ROUTE_VMEM),
        name="outproj_norm",
    )(y, w, x, g.reshape(1, d))


def router(hn, rw2, *, tm=ROUTER_TM):
    t, d = hn.shape
    tm = min(tm, t)
    return pl.pallas_call(
        _router_kernel,
        grid=(t // tm,),
        in_specs=[pl.BlockSpec((tm, d), lambda i: (i, 0)), pl.BlockSpec((2 * d, 2 * LANES), lambda i: (0, 0))],
        out_specs=[pl.BlockSpec((tm, LANES), lambda i: (i, 0)), pl.BlockSpec((SUBLANES, LANES), lambda i: (0, 0))],
        out_shape=[jax.ShapeDtypeStruct((t, LANES), F32), jax.ShapeDtypeStruct((SUBLANES, LANES), F32)],
        scratch_shapes=[pltpu.VMEM((SUBLANES, LANES), F32)],
        compiler_params=_params(("arbitrary",), ROUTE_VMEM),
        name="router",
    )(hn, rw2)


def _swiglu_step(xb, wg, wu, wd):
    g = jnp.dot(xb, wg, preferred_element_type=F32)
    u = jnp.dot(xb, wu, preferred_element_type=F32)
    a = (g * jax.nn.sigmoid(g) * u).astype(BF16)
    return jnp.dot(a, wd, preferred_element_type=F32)


W_SLOTS = 3


def _weight_copies(wg_hbm, wu_hbm, wd_hbm, f, tf, wgr, wur, wdr, sem, slot, expert=None):
    cols = pl.ds(pl.multiple_of(f * tf, tf), tf)
    if expert is not None:
        wg_hbm, wu_hbm, wd_hbm = wg_hbm.at[expert], wu_hbm.at[expert], wd_hbm.at[expert]
    return (pltpu.make_async_copy(wg_hbm.at[:, cols], wgr.at[slot], sem.at[0, slot]),
            pltpu.make_async_copy(wu_hbm.at[:, cols], wur.at[slot], sem.at[1, slot]),
            pltpu.make_async_copy(wd_hbm.at[cols, :], wdr.at[slot], sem.at[2, slot]))


def _ffn_kernel(x_ref, g_ref, wg_hbm, wu_hbm, wd_hbm, xo_ref, hb_ref, wgr, wur, wdr, sem):
    i = pl.program_id(0)
    f = pl.program_id(1)
    nf = pl.num_programs(1)
    tf = wgr.shape[2]
    step = i * nf + f
    last = pl.num_programs(0) * nf - 1

    def fetch(s):
        for cp in _weight_copies(wg_hbm, wu_hbm, wd_hbm, lax.rem(s, nf), tf, wgr, wur, wdr, sem,
                                 lax.rem(s, W_SLOTS)):
            cp.start()

    @pl.when(step == 0)
    def _():
        fetch(step)
        fetch(step + 1)

    @pl.when(step + 2 <= last)
    def _():
        fetch(step + 2)

    @pl.when(f == 0)
    def _():
        x = x_ref[...]
        hb_ref[...] = _rms(x, g_ref[...]).astype(BF16)
        xo_ref[...] = x

    slot = lax.rem(step, W_SLOTS)
    for cp in _weight_copies(wg_hbm, wu_hbm, wd_hbm, f, tf, wgr, wur, wdr, sem, slot):
        cp.wait()
    xo_ref[...] += _swiglu_step(hb_ref[...], wgr[slot].astype(BF16), wur[slot].astype(BF16),
                                wdr[slot].astype(BF16))


def ffn_dense(x, g, wg, wu, wd, *, tm=FFN_TM, tf=FFN_TF):
    t, d = x.shape
    ff = wg.shape[1]
    tm = min(tm, t)
    rowb = pl.BlockSpec((tm, d), lambda i, f: (i, 0))
    return pl.pallas_call(
        _ffn_kernel,
        grid=(t // tm, ff // tf),
        in_specs=[rowb, pl.BlockSpec((1, d), lambda i, f: (0, 0)),
                  pl.BlockSpec(memory_space=pl.ANY), pl.BlockSpec(memory_space=pl.ANY),
                  pl.BlockSpec(memory_space=pl.ANY)],
        out_specs=rowb,
        out_shape=jax.ShapeDtypeStruct((t, d), F32),
        scratch_shapes=[pltpu.VMEM((tm, d), BF16), pltpu.VMEM((W_SLOTS, d, tf), F32),
                        pltpu.VMEM((W_SLOTS, d, tf), F32), pltpu.VMEM((W_SLOTS, tf, d), F32),
                        pltpu.SemaphoreType.DMA((3, W_SLOTS))],
        compiler_params=_params(("arbitrary", "arbitrary"), FFN_VMEM),
        name="ffn_dense",
    )(x, g.reshape(1, d), wg, wu, wd)


def _row_copy(src, src_row, dst, dst_row, sem):
    return pltpu.make_async_copy(src.at[pl.ds(src_row, 1)], dst.at[pl.ds(dst_row, 1)], sem)


def _for_rows(n, fn):
    def body(r, c):
        fn(r)
        return c

    lax.fori_loop(0, n, body, 0)


M_ALWAYS = 512
M_CHUNK = 256


def _expert_kernel(te_ref, tb_ref, nv_ref, xs_ref, wg_hbm, wu_hbm, wd_hbm, ys_ref, xb, wgr, wur, wdr, sem):
    i = pl.program_id(0)
    f = pl.program_id(1)
    nt = pl.num_programs(0)
    nf = pl.num_programs(1)
    tm = xb.shape[0]
    tf = wgr.shape[2]
    nv = nv_ref[i]
    step = i * nf + f

    def copies(tile, fs, s):
        return _weight_copies(wg_hbm, wu_hbm, wd_hbm, fs, tf, wgr, wur, wdr, sem, lax.rem(s, W_SLOTS),
                              expert=te_ref[tile])

    def fetch_ahead(k):
        over = f + k >= nf
        tile = jnp.minimum(i + over.astype(jnp.int32), nt - 1)
        fs = f + k - nf * over.astype(jnp.int32)

        @pl.when(jnp.logical_not(over) | ((i + 1 < nt) & (nv_ref[tile] > 0)))
        def _():
            for cp in copies(tile, fs, step + k):
                cp.start()

    @pl.when(nv > 0)
    def _():
        @pl.when(step == 0)
        def _():
            fetch_ahead(0)
            fetch_ahead(1)

        fetch_ahead(2)

        @pl.when(f == 0)
        def _():
            xb[...] = xs_ref[...].astype(BF16)
            ys_ref[...] = jnp.zeros_like(ys_ref)

        slot = lax.rem(step, W_SLOTS)
        for cp in copies(i, f, step):
            cp.wait()
        m0 = min(M_ALWAYS, tm)

        @pl.when(nv > tm - M_CHUNK)
        def _():
            ys_ref[...] += _swiglu_step(xb[...], wgr[slot].astype(BF16), wur[slot].astype(BF16),
                                        wdr[slot].astype(BF16))

        @pl.when(nv <= tm - M_CHUNK)
        def _():
            ys_ref[0:m0, :] += _swiglu_step(xb[0:m0, :], wgr[slot].astype(BF16), wur[slot].astype(BF16),
                                            wdr[slot].astype(BF16))
            for c in range(m0, tm - M_CHUNK, M_CHUNK):
                @pl.when(c < nv)
                def _():
                    rows = pl.ds(c, M_CHUNK)
                    ys_ref[rows, :] += _swiglu_step(xb[rows, :], wgr[slot].astype(BF16),
                                                    wur[slot].astype(BF16), wdr[slot].astype(BF16))

    @pl.when((nv == 0) & (f == nf - 1))
    def _():
        ys_ref[...] = jnp.zeros_like(ys_ref)


def moe_experts(xs, wg, wu, wd, tile_expert, tile_block, tile_rows, *, tm, tf=MOE_TF):
    slots, d = xs.shape
    fe = wg.shape[2]
    nf = fe // tf
    nt = slots // tm

    grid_spec = pltpu.PrefetchScalarGridSpec(
        num_scalar_prefetch=3,
        grid=(nt, nf),
        in_specs=[
            pl.BlockSpec((tm, d), lambda i, f, te, tb, nv: (tb[i], 0)),
            pl.BlockSpec(memory_space=pl.ANY), pl.BlockSpec(memory_space=pl.ANY),
            pl.BlockSpec(memory_space=pl.ANY),
        ],
        out_specs=pl.BlockSpec((tm, d), lambda i, f, te, tb, nv: (i, 0)),
        scratch_shapes=[pltpu.VMEM((tm, d), BF16), pltpu.VMEM((W_SLOTS, d, tf), F32),
                        pltpu.VMEM((W_SLOTS, d, tf), F32), pltpu.VMEM((W_SLOTS, tf, d), F32),
                        pltpu.SemaphoreType.DMA((3, W_SLOTS))],
    )
    return pl.pallas_call(
        _expert_kernel,
        grid_spec=grid_spec,
        out_shape=jax.ShapeDtypeStruct((slots, d), F32),
        compiler_params=_params(("arbitrary", "arbitrary"), MOE_VMEM),
        name="moe_experts",
    )(tile_expert, tile_block, tile_rows, xs, wg, wu, wd)


def _dispatch_kernel(pos_ref, trows_ref, h_ref, xs_hbm, zbuf, sem, zsem):
    tmd = h_ref.shape[0]
    tm = zbuf.shape[0]

    @pl.when(pl.program_id(0) == 0)
    def _():
        zbuf[...] = jnp.zeros_like(zbuf)

        def tile_fill(j):
            return pltpu.make_async_copy(zbuf, xs_hbm.at[pl.ds(j * tm, tm)], zsem)

        def fill(j):
            @pl.when(trows_ref[j] < tm)
            def _():
                tile_fill(j).start()

        def fill_wait(j):
            @pl.when(trows_ref[j] < tm)
            def _():
                tile_fill(j).wait()

        _for_rows(trows_ref.shape[0], fill)
        _for_rows(trows_ref.shape[0], fill_wait)

    for tok in range(tmd):
        for k in range(2):
            _row_copy(h_ref, tok, xs_hbm, pos_ref[2 * tok + k], sem).start(priority=k)
    for _ in range(2):
        pltpu.make_async_copy(h_ref, xs_hbm.at[pl.ds(0, tmd)], sem).wait()


def moe_dispatch(hn, pos, tile_rows, *, tm, tmd=DISPATCH_TM):
    t, d = hn.shape
    tmd = min(tmd, t)
    return pl.pallas_call(
        _dispatch_kernel,
        grid=(t // tmd,),
        in_specs=[pl.BlockSpec((2 * tmd,), lambda i: (i,), memory_space=pltpu.SMEM),
                  pl.BlockSpec(memory_space=pltpu.SMEM),
                  pl.BlockSpec((tmd, d), lambda i: (i, 0))],
        out_specs=pl.BlockSpec(memory_space=pl.ANY),
        out_shape=jax.ShapeDtypeStruct((tile_rows.shape[0] * tm, d), hn.dtype),
        scratch_shapes=[pltpu.VMEM((tm, d), hn.dtype), pltpu.SemaphoreType.DMA(()),
                        pltpu.SemaphoreType.DMA(())],
        compiler_params=_params(("arbitrary",), DISPATCH_VMEM),
        name="moe_dispatch",
    )(pos, tile_rows, hn)


def _combine_kernel(pos_ref, posn_ref, route_ref, x_ref, g_ref, ys_hbm, o_ref, buf, sem):
    i = pl.program_id(0)
    n = pl.num_programs(0)
    tmc = x_ref.shape[0]

    def gather(pref, slot):
        for tok in range(tmc):
            for k in range(2):
                _row_copy(ys_hbm, pref[2 * tok + k], buf.at[slot, k], tok, sem.at[slot]).start(priority=k)

    @pl.when(i == 0)
    def _():
        gather(pos_ref, 0)

    @pl.when(i + 1 < n)
    def _():
        gather(posn_ref, (i + 1) % 2)

    cur = i % 2
    for k in range(2):
        pltpu.make_async_copy(ys_hbm.at[pl.ds(0, tmc)], buf.at[cur, k], sem.at[cur]).wait()
    xn = x_ref[...] + (route_ref[:, 4:5] * buf[cur, 0] + route_ref[:, 5:6] * buf[cur, 1])
    o_ref[...] = _rms(xn, g_ref[...]).astype(o_ref.dtype)


def moe_combine(ys, pos, route, x, g, *, tmc=COMBINE_TM):
    t, d = x.shape
    tmc = min(tmc, t)
    n = t // tmc
    rowb = pl.BlockSpec((tmc, d), lambda i: (i, 0))
    cur = pl.BlockSpec((2 * tmc,), lambda i: (i,), memory_space=pltpu.SMEM)
    nxt = pl.BlockSpec((2 * tmc,), lambda i: (jnp.minimum(i + 1, n - 1),), memory_space=pltpu.SMEM)
    return pl.pallas_call(
        _combine_kernel,
        grid=(n,),
        in_specs=[cur, nxt,
                  pl.BlockSpec((tmc, LANES), lambda i: (i, 0)), rowb, pl.BlockSpec((1, d), lambda i: (0, 0)),
                  pl.BlockSpec(memory_space=pl.ANY)],
        out_specs=rowb,
        out_shape=jax.ShapeDtypeStruct((t, d), F32),
        scratch_shapes=[pltpu.VMEM((2, 2, tmc, d), F32), pltpu.SemaphoreType.DMA((2,))],
        compiler_params=_params(("arbitrary",), COMBINE_VMEM),
        name="moe_combine",
    )(pos, pos, route, x, g.reshape(1, d), ys)


def _tile_schedule(counts, tm, nt):
    cnt = counts.astype(jnp.int32)
    ntile = (cnt + tm - 1) // tm
    tend = jnp.cumsum(ntile)
    tstart = tend - ntile
    total = tend[-1]
    tid = jnp.arange(nt, dtype=jnp.int32)
    active = tid < total
    tclamp = jnp.minimum(tid, total - 1)
    te = jnp.sum((tclamp[:, None] >= tend[None, :]).astype(jnp.int32), axis=1)
    rows = jnp.clip(cnt[te] - (tclamp - tstart[te]) * tm, 0, tm)
    return tstart * tm, te, tclamp, jnp.where(active, rows, 0)


def moe_block(hn, route, counts, x, g, wg, wu, wd, *, tm=MOE_TM):
    t, d = hn.shape
    nt = -(-2 * t // tm) + N_EXPERTS
    off, te, tblk, trows = _tile_schedule(counts[0, :N_EXPERTS], tm, nt)
    pos = (off[route[:, 0:2].astype(jnp.int32)] + route[:, 2:4].astype(jnp.int32)).reshape(-1)
    xs = moe_dispatch(hn, pos, trows, tm=tm)
    ys = moe_experts(xs, wg, wu, wd, te, tblk, trows, tm=tm)
    return moe_combine(ys, pos, route, x, g)


def _block_diag(w):
    h, n, _ = w.shape
    eye = jnp.eye(h, dtype=w.dtype)
    return (eye[:, None, :, None] * w[:, :, None, :]).reshape(h * n, h * n)


def kernel(x, attn_norm, w_in, attn_sinks, conv_w, lru_conv_w, lru_conv_b, lru_wa, lru_ba, lru_wx,
           lru_bx, lru_lambda, mix_norm, w_out, ffn_norm, dense_w_gate, dense_w_up, dense_w_down,
           router_w, expert_w_gate, expert_w_up, expert_w_down, final_norm):
    b, s, d = x.shape
    depth = w_in.shape[0]
    assert depth == 2 and d == D_MODEL
    xf = x.reshape(b * s, d)
    out = None
    for layer in range(depth):
        proj = inproj(xf, attn_norm[layer], w_in[layer].astype(BF16))
        y = mixer(proj, attn_sinks[layer], conv_w[layer], lru_conv_w[layer], lru_conv_b[layer],
                  _block_diag(lru_wa[layer]).astype(BF16), lru_ba[layer],
                  _block_diag(lru_wx[layer]).astype(BF16), lru_bx[layer], lru_lambda[layer],
                  mix_norm[layer], batch=b)
        j = layer // 2
        if layer % 2 == 0:
            xf = outproj(y, w_out[layer].astype(BF16), xf)
            xf = ffn_dense(xf, ffn_norm[layer], dense_w_gate[j], dense_w_up[j], dense_w_down[j])
        else:
            rw = jnp.pad(router_w[j], ((0, 0), (0, LANES - N_EXPERTS)))
            rw_hi = rw.astype(BF16)
            rw_lo = (rw - rw_hi.astype(F32)).astype(BF16)
            rw2 = jnp.concatenate([jnp.concatenate([rw_hi, rw_lo], axis=1),
                                   jnp.concatenate([rw_hi, jnp.zeros_like(rw_lo)], axis=1)], axis=0)
            xf, hn = outproj_norm(y, w_out[layer].astype(BF16), xf, ffn_norm[layer])
            route, counts = router(hn, rw2)
            out = moe_block(hn, route, counts, xf, final_norm, expert_w_gate[j], expert_w_up[j],
                            expert_w_down[j])
    return out.reshape(b, s, d)
```
